```python
import math
import jax
import jax.numpy as jnp
from jax import lax
import numpy as np

D_MODEL = 2048
BATCH = 2
SEQ = 16384
DEPTH = 1

DA_HEADS = 8
DA_QK_DIM = 64
DA_V_DIM = 2 * DA_QK_DIM
GDN_HEADS = 8
GDN_DK = 128
GDN_DV = 128
GDN_CONV = 5
GDN_CHUNK = 64
Q_BLOCK = 128
REL_BUCKETS = 32
REL_MAX_DIST = 128
MEM_LEN = 256
CA_HEADS = 4
CA_HEAD_DIM = 128
N_EXPERTS = 16
EXPERT_FF = 2048
CAPACITY = 2
NORM_EPS = 1e-6

DA_QK_COLS = DA_HEADS * 2 * DA_QK_DIM
DA_V_COLS = DA_HEADS * DA_V_DIM
GDN_QK_COLS = GDN_HEADS * GDN_DK
GDN_V_COLS = GDN_HEADS * GDN_DV
MIX_WIDTH = DA_V_COLS + GDN_V_COLS
IN_SPLIT_SIZES = (DA_QK_COLS, DA_QK_COLS, DA_V_COLS,
                  GDN_QK_COLS, GDN_QK_COLS, GDN_V_COLS, GDN_V_COLS,
                  GDN_HEADS, GDN_HEADS, GDN_HEADS, GDN_HEADS)
IN_COLS = sum(IN_SPLIT_SIZES)

kernel_name = 'hybrid_diffattn_gdn_ec_moe_encoder'


def rms_norm(x, w, eps=NORM_EPS):
    xf = x.astype(jnp.float32)
    y = xf * lax.rsqrt(jnp.mean(xf * xf, axis=-1, keepdims=True) + eps)
    return (y * w.astype(jnp.float32)).astype(x.dtype)


def l2_normalize(t, eps=1e-6):
    return t * lax.rsqrt(jnp.sum(t * t, axis=-1, keepdims=True) + eps)


def t5_bucket(rel):
    half = REL_BUCKETS // 2
    max_exact = half // 2
    n = jnp.abs(rel)
    large = max_exact + (jnp.log(jnp.maximum(n, 1).astype(jnp.float32) / max_exact)
                         / math.log(REL_MAX_DIST / max_exact) * (half - max_exact)).astype(jnp.int32)
    large = jnp.minimum(large, half - 1)
    return jnp.where(rel > 0, half, 0) + jnp.where(n < max_exact, n, large)


def short_conv(u, w):
    k = w.shape[0]
    return lax.conv_general_dilated(u, w[:, None, :].astype(u.dtype), window_strides=(1,),
                                    padding=[((k - 1) // 2, k // 2)],
                                    dimension_numbers=('NWC', 'WIO', 'NWC'),
                                    feature_group_count=u.shape[-1])


def diff_attention(q1, q2, k1, k2, v, lam, bias_table):
    b, h, s, dh = q1.shape
    nb = s // Q_BLOCK
    scale = dh ** -0.5
    kpos = jnp.arange(s, dtype=jnp.int32)
    table = bias_table.astype(jnp.float32)

    def to_blocks(t):
        return jnp.moveaxis(t.reshape(b, h, nb, Q_BLOCK, dh), 2, 0)

    def one_block(args):
        qb1, qb2, q0 = args
        qpos = q0 + jnp.arange(Q_BLOCK, dtype=jnp.int32)
        bias = jnp.transpose(table[t5_bucket(kpos[None, :] - qpos[:, None])], (2, 0, 1))
        s1 = jnp.einsum('bhqd,bhkd->bhqk', qb1, k1).astype(jnp.float32) * scale + bias
        s2 = jnp.einsum('bhqd,bhkd->bhqk', qb2, k2).astype(jnp.float32) * scale + bias
        p = jax.nn.softmax(s1, axis=-1) - lam * jax.nn.softmax(s2, axis=-1)
        return jnp.einsum('bhqk,bhkd->bhqd', p.astype(v.dtype), v)

    starts = jnp.arange(nb, dtype=jnp.int32) * Q_BLOCK
    o = lax.map(one_block, (to_blocks(q1), to_blocks(q2), starts))
    return jnp.moveaxis(o, 0, 2).reshape(b, h, s, v.shape[-1])


def gated_delta_chunked(q, k, v, g, beta):
    b, h, s, dk = q.shape
    dv = v.shape[-1]
    c = GDN_CHUNK
    n = s // c
    q = (q * dk ** -0.5).reshape(b, h, n, c, dk)
    k = k.reshape(b, h, n, c, dk)
    v = v.reshape(b, h, n, c, dv)
    beta = beta.reshape(b, h, n, c, 1)
    gc = jnp.cumsum(g.reshape(b, h, n, c), axis=-1)
    incl = jnp.tril(jnp.ones((c, c), dtype=bool))
    strict = jnp.tril(jnp.ones((c, c), dtype=bool), -1)
    decay = jnp.where(incl, jnp.exp(jnp.where(incl, gc[..., :, None] - gc[..., None, :], 0.0)), 0.0)
    kb = k * beta
    lower = jnp.where(strict, jnp.einsum('bhnid,bhnjd->bhnij', kb, k) * decay, 0.0)
    wy = lower + jnp.eye(c, dtype=lower.dtype)
    u = lax.linalg.triangular_solve(wy, v * beta, left_side=True, lower=True, unit_diagonal=True)
    w = lax.linalg.triangular_solve(wy, kb * jnp.exp(gc)[..., None], left_side=True, lower=True,
                                    unit_diagonal=True)
    intra = jnp.where(incl, jnp.einsum('bhnid,bhnjd->bhnij', q, k) * decay, 0.0)
    q_dec = q * jnp.exp(gc)[..., None]
    k_tail = k * jnp.exp(gc[..., -1:] - gc)[..., None]
    chunk_decay = jnp.exp(gc[..., -1])

    def step(state, xs):
        u_i, w_i, qd_i, kt_i, a_i, cd_i = xs
        v_new = u_i - jnp.einsum('bhck,bhkv->bhcv', w_i, state)
        o_i = jnp.einsum('bhck,bhkv->bhcv', qd_i, state) + jnp.einsum('bhij,bhjv->bhiv', a_i, v_new)
        state = state * cd_i[..., None, None] + jnp.einsum('bhck,bhcv->bhkv', kt_i, v_new)
        return state, o_i

    xs = tuple(jnp.moveaxis(t, 2, 0) for t in (u, w, q_dec, k_tail, intra, chunk_decay))
    _, o = lax.scan(step, jnp.zeros((b, h, dk, dv), jnp.float32), xs)
    return jnp.moveaxis(o, 0, 2).reshape(b, h, s, dv)


def memory_cross_attention(hx, hm, w_cq, w_ckv, w_co):
    b, s, _ = hx.shape
    q = jnp.einsum('bsd,de->bse', hx, w_cq).reshape(b, s, CA_HEADS, CA_HEAD_DIM)
    kv = jnp.einsum('bmd,de->bme', hm, w_ckv).reshape(b, hm.shape[1], 2, CA_HEADS, CA_HEAD_DIM)
    sc = jnp.einsum('bshd,bmhd->bhsm', q, kv[:, :, 0]).astype(jnp.float32) * CA_HEAD_DIM ** -0.5
    p = jax.nn.softmax(sc, axis=-1).astype(hx.dtype)
    o = jnp.einsum('bhsm,bmhd->bshd', p, kv[:, :, 1]).reshape(b, s, CA_HEADS * CA_HEAD_DIM)
    return jnp.einsum('bse,ed->bsd', o, w_co)


def expert_choice_moe(h, w_router, w_gate, w_up, w_down):
    b, s, d = h.shape
    cap = CAPACITY * s // N_EXPERTS
    aff = jax.nn.softmax(jnp.einsum('bsd,de->bse', h, w_router).astype(jnp.float32), axis=-1)
    gate, idx = lax.top_k(jnp.swapaxes(aff, 1, 2), cap)
    xs = jax.vmap(lambda hb, ib: hb[ib])(h, idx)
    a = jnp.einsum('becd,edf->becf', xs, w_gate)
    up = jnp.einsum('becd,edf->becf', xs, w_up)
    y = jnp.einsum('becf,efd->becd', jax.nn.silu(a) * up, w_down) * gate[..., None].astype(h.dtype)
    return jax.vmap(lambda ib, yb: jnp.zeros((s, d), yb.dtype).at[ib.reshape(-1)].add(yb.reshape(-1, d)))(idx, y)


def setup_inputs(seed: int = 0) -> dict:
    key = jax.random.key(seed)
    ks = jax.random.split(key, 32)
    f32 = jnp.float32
    L = DEPTH

    def nrm(k, shape, scale):
        return jax.random.normal(k, shape, f32) * scale

    def gain(k, shape):
        return 1.0 + 0.02 * jax.random.normal(k, shape, f32)

    a_init = jax.random.uniform(ks[12], (L, 2, GDN_HEADS), f32, 1.0, 16.0)
    dt = jnp.exp(jax.random.uniform(ks[13], (L, 2, GDN_HEADS), f32, math.log(1e-3), math.log(1e-1)))
    return {
        'x': nrm(ks[0], (BATCH, SEQ, D_MODEL), 1.0),
        'mem': nrm(ks[1], (BATCH, MEM_LEN, D_MODEL), 1.0),
        'rel_bias_table': nrm(ks[2], (REL_BUCKETS, DA_HEADS), 0.5),
        'norm_mix': gain(ks[3], (L, D_MODEL)),
        'w_in': nrm(ks[4], (L, D_MODEL, IN_COLS), D_MODEL ** -0.5),
        'conv_w': nrm(ks[5], (L, GDN_CONV, 2 * GDN_QK_COLS + GDN_V_COLS), GDN_CONV ** -0.5),
        'lambda_q1': nrm(ks[6], (L, DA_QK_DIM), 0.1),
        'lambda_k1': nrm(ks[7], (L, DA_QK_DIM), 0.1),
        'lambda_q2': nrm(ks[8], (L, DA_QK_DIM), 0.1),
        'lambda_k2': nrm(ks[9], (L, DA_QK_DIM), 0.1),
        'da_subln': gain(ks[10], (L, DA_V_DIM)),
        'gdn_a_log': jnp.log(a_init),
        'gdn_dt_bias': dt + jnp.log(-jnp.expm1(-dt)),
        'gdn_norm': gain(ks[11], (L, GDN_DV)),
        'w_out': nrm(ks[14], (L, MIX_WIDTH, D_MODEL), MIX_WIDTH ** -0.5),
        'norm_cross': gain(ks[15], (L, D_MODEL)),
        'norm_mem': gain(ks[16], (L, D_MODEL)),
        'w_cq': nrm(ks[17], (L, D_MODEL, CA_HEADS * CA_HEAD_DIM), D_MODEL ** -0.5),
        'w_ckv': nrm(ks[18], (L, D_MODEL, 2 * CA_HEADS * CA_HEAD_DIM), D_MODEL ** -0.5),
        'w_co': nrm(ks[19], (L, CA_HEADS * CA_HEAD_DIM, D_MODEL), (CA_HEADS * CA_HEAD_DIM) ** -0.5),
        'norm_moe': gain(ks[20], (L, D_MODEL)),
        'w_router': nrm(ks[21], (L, D_MODEL, N_EXPERTS), D_MODEL ** -0.5),
        'w_gate': nrm(ks[22], (L, N_EXPERTS, D_MODEL, EXPERT_FF), D_MODEL ** -0.5),
        'w_up': nrm(ks[23], (L, N_EXPERTS, D_MODEL, EXPERT_FF), D_MODEL ** -0.5),
        'w_down': nrm(ks[24], (L, N_EXPERTS, EXPERT_FF, D_MODEL), EXPERT_FF ** -0.5),
        'norm_final': gain(ks[25], (D_MODEL,)),
    }


def reference(x, mem, rel_bias_table, norm_mix, w_in, conv_w, lambda_q1, lambda_k1, lambda_q2, lambda_k2,
              da_subln, gdn_a_log, gdn_dt_bias, gdn_norm, w_out, norm_cross, norm_mem, w_cq, w_ckv, w_co,
              norm_moe, w_router, w_gate, w_up, w_down, norm_final):
    f32 = jnp.float32
    b, s, _ = x.shape
    split_at = np.cumsum(IN_SPLIT_SIZES)[:-1].tolist()

    def to_heads(t, n_heads, dim):
        return jnp.transpose(t.reshape(b, s, n_heads, dim), (0, 2, 1, 3))

    for l in range(DEPTH):
        lam_init = 0.8 - 0.6 * math.exp(-0.3 * l)
        h = rms_norm(x, norm_mix[l])
        proj = jnp.einsum('bsd,de->bse', h, w_in[l])
        da_q, da_k, da_v, g_q, g_k, g_v, g_z, a_f, b_f, a_b, b_b = jnp.split(proj, split_at, axis=-1)

        dq = to_heads(da_q, DA_HEADS, 2 * DA_QK_DIM)
        dk = to_heads(da_k, DA_HEADS, 2 * DA_QK_DIM)
        lam = (jnp.exp(jnp.sum(lambda_q1[l].astype(f32) * lambda_k1[l].astype(f32)))
               - jnp.exp(jnp.sum(lambda_q2[l].astype(f32) * lambda_k2[l].astype(f32))) + lam_init)
        o_da = diff_attention(dq[..., :DA_QK_DIM], dq[..., DA_QK_DIM:], dk[..., :DA_QK_DIM], dk[..., DA_QK_DIM:],
                              to_heads(da_v, DA_HEADS, DA_V_DIM), lam, rel_bias_table)
        o_da = rms_norm(jnp.transpose(o_da, (0, 2, 1, 3)), da_subln[l]) * (1.0 - lam_init)
        o_da = o_da.reshape(b, s, DA_V_COLS)

        qkv = jax.nn.silu(short_conv(jnp.concatenate([g_q, g_k, g_v], axis=-1), conv_w[l]))
        c_q, c_k, c_v = jnp.split(qkv, [GDN_QK_COLS, 2 * GDN_QK_COLS], axis=-1)
        gq = l2_normalize(to_heads(c_q, GDN_HEADS, GDN_DK).astype(f32))
        gk = l2_normalize(to_heads(c_k, GDN_HEADS, GDN_DK).astype(f32))
        gv = to_heads(c_v, GDN_HEADS, GDN_DV).astype(f32)
        log_a_f = jnp.swapaxes(-jnp.exp(gdn_a_log[l, 0].astype(f32))
                               * jax.nn.softplus(a_f.astype(f32) + gdn_dt_bias[l, 0].astype(f32)), 1, 2)
        log_a_b = jnp.swapaxes(-jnp.exp(gdn_a_log[l, 1].astype(f32))
                               * jax.nn.softplus(a_b.astype(f32) + gdn_dt_bias[l, 1].astype(f32)), 1, 2)
        beta_f = jnp.swapaxes(jax.nn.sigmoid(b_f.astype(f32)), 1, 2)
        beta_b = jnp.swapaxes(jax.nn.sigmoid(b_b.astype(f32)), 1, 2)
        o_fwd = gated_delta_chunked(gq, gk, gv, log_a_f, beta_f)
        o_bwd = jnp.flip(gated_delta_chunked(jnp.flip(gq, 2), jnp.flip(gk, 2), jnp.flip(gv, 2),
                                             jnp.flip(log_a_b, 2), jnp.flip(beta_b, 2)), 2)
        o_gdn = jnp.transpose(o_fwd + o_bwd, (0, 2, 1, 3))
        z = g_z.reshape(b, s, GDN_HEADS, GDN_DV).astype(f32)
        o_gdn = (rms_norm(o_gdn, gdn_norm[l]) * jax.nn.silu(z)).astype(x.dtype).reshape(b, s, GDN_V_COLS)

        x = x + jnp.einsum('bsm,md->bsd', jnp.concatenate([o_da, o_gdn], axis=-1), w_out[l])

        x = x + memory_cross_attention(rms_norm(x, norm_cross[l]), rms_norm(mem, norm_mem[l]),
                                       w_cq[l], w_ckv[l], w_co[l])

        x = x + expert_choice_moe(rms_norm(x, norm_moe[l]), w_router[l], w_gate[l], w_up[l], w_down[l])

    return rms_norm(x, norm_final)
```

```python
import functools
import math

import jax
import jax.numpy as jnp
from jax import lax
from jax.experimental import pallas as pl
from jax.experimental.pallas import tpu as pltpu

F32 = jnp.float32
BF16 = jnp.bfloat16

DA_HEADS = 8
DA_QK_DIM = 64
DA_V_DIM = 128
GDN_HEADS = 8
GDN_D = 128
GDN_CONV = 5
GDN_CHUNK = 64
REL_BUCKETS = 32
CA_HEADS = 4
CA_HEAD_DIM = 128
N_EXPERTS = 16
CAPACITY = 2
NORM_EPS = 1e-6
L2_EPS = 1e-6
LAM_INIT = 0.8 - 0.6 * math.exp(-0.3 * 0)
LOG2E = math.log2(math.e)
NEG_BIG = -1e30

LANES = 128
ATT_TILE = 512
VMEM_LIMIT = 56 * 1024 * 1024


def _cparams(sem):
    return pltpu.CompilerParams(dimension_semantics=sem, vmem_limit_bytes=VMEM_LIMIT)


def _rms(x, w):
    return x * lax.rsqrt(jnp.mean(x * x, axis=-1, keepdims=True) + NORM_EPS) * w


def _norm_matmul_kernel(x_ref, nw_ref, w_ref, cs_ref, o_ref, hn_ref):
    @pl.when(pl.program_id(1) == 0)
    def _():
        hn_ref[...] = _rms(x_ref[...], nw_ref[...]).astype(BF16)

    acc = jnp.dot(hn_ref[...], w_ref[...], preferred_element_type=F32)
    o_ref[...] = (acc * cs_ref[...]).astype(o_ref.dtype)


def _norm_matmul(x2d, norm_w, w_bf16, colscale, tm, tn, out_dtype):
    m, d = x2d.shape
    n = w_bf16.shape[1]
    return pl.pallas_call(
        _norm_matmul_kernel,
        grid=(m // tm, n // tn),
        in_specs=[
            pl.BlockSpec((tm, d), lambda i, j: (i, 0)),
            pl.BlockSpec((1, d), lambda i, j: (0, 0)),
            pl.BlockSpec((d, tn), lambda i, j: (0, j)),
            pl.BlockSpec((1, tn), lambda i, j: (0, j)),
        ],
        out_specs=pl.BlockSpec((tm, tn), lambda i, j: (i, j)),
        out_shape=jax.ShapeDtypeStruct((m, n), out_dtype),
        scratch_shapes=[pltpu.VMEM((tm, d), BF16)],
        compiler_params=_cparams(("parallel", "arbitrary")),
        name="norm_matmul",
    )(x2d, norm_w.reshape(1, d), w_bf16, colscale.reshape(1, n))


def _gates_kernel(x_ref, nw_ref, w_ref, par_ref, o_ref):
    tm = x_ref.shape[0]
    hn = _rms(x_ref[...], nw_ref[...])
    pre = jnp.dot(hn, w_ref[...], preferred_element_type=F32, precision=lax.Precision.HIGHEST)
    a_log = par_ref[0:1, :]
    dt_bias = par_ref[1:2, :]
    col = lax.broadcasted_iota(jnp.int32, (1, LANES), 1)
    is_decay = (col < GDN_HEADS) | ((col >= 2 * GDN_HEADS) & (col < 3 * GDN_HEADS))
    is_bwd = col >= 2 * GDN_HEADS
    z = pre + dt_bias
    softplus = jnp.maximum(z, 0.0) + jnp.log(1.0 + jnp.exp(-jnp.abs(z)))
    log_a = -jnp.exp(a_log) * softplus
    beta = 1.0 / (1.0 + jnp.exp(-pre))
    r = lax.broadcasted_iota(jnp.int32, (tm, tm), 0)
    c = lax.broadcasted_iota(jnp.int32, (tm, tm), 1)
    same = (r // GDN_CHUNK) == (c // GDN_CHUNK)
    tri_f = jnp.where(same & (c <= r), 1.0, 0.0).astype(F32)
    tri_b = jnp.where(same & (c >= r), 1.0, 0.0).astype(F32)
    la = jnp.where(is_decay, log_a, 0.0)
    cum_f = jnp.dot(tri_f, la, preferred_element_type=F32, precision=lax.Precision.HIGHEST)
    cum_b = jnp.dot(tri_b, la, preferred_element_type=F32, precision=lax.Precision.HIGHEST)
    cum = jnp.where(is_bwd, cum_b, cum_f)
    o_ref[...] = jnp.where(is_decay, cum, beta)


def _gates(x2d, norm_w, w_gates, par, tm):
    m, d = x2d.shape
    return pl.pallas_call(
        _gates_kernel,
        grid=(m // tm,),
        in_specs=[
            pl.BlockSpec((tm, d), lambda i: (i, 0)),
            pl.BlockSpec((1, d), lambda i: (0, 0)),
            pl.BlockSpec((d, LANES), lambda i: (0, 0)),
            pl.BlockSpec((8, LANES), lambda i: (0, 0)),
        ],
        out_specs=pl.BlockSpec((tm, LANES), lambda i: (i, 0)),
        out_shape=jax.ShapeDtypeStruct((m, LANES), F32),
        compiler_params=_cparams(("parallel",)),
        name="gdn_gates",
    )(x2d, norm_w.reshape(1, d), w_gates, par)


_T5_THRESH = (12, 16, 23, 32, 46, 64, 91)


def _bias_kernel(table_ref, o_ref, *, t):
    h = pl.program_id(0)
    d = pl.program_id(1)
    r = lax.broadcasted_iota(jnp.int32, (t, t), 0)
    c = lax.broadcasted_iota(jnp.int32, (t, t), 1)
    rel = (d - 2) * t + c - r
    n = jnp.abs(rel)
    large = jnp.full((t, t), 8, jnp.int32)
    for th in _T5_THRESH:
        large = large + jnp.where(n >= th, 1, 0)
    bucket = jnp.where(rel > 0, 16, 0) + jnp.where(n < 8, n, large)
    bias = jnp.zeros((t, t), F32)
    for b in range(REL_BUCKETS):
        bias = jnp.where(bucket == b, table_ref[b, h], bias)
    o_ref[0, 0] = bias * LOG2E


def _bias_tiles(table, t):
    return pl.pallas_call(
        functools.partial(_bias_kernel, t=t),
        grid=(DA_HEADS, 5),
        in_specs=[pl.BlockSpec(memory_space=pltpu.SMEM)],
        out_specs=pl.BlockSpec((1, 1, t, t), lambda h, d: (h, d, 0, 0)),
        out_shape=jax.ShapeDtypeStruct((DA_HEADS, 5, t, t), F32),
        compiler_params=_cparams(("parallel", "parallel")),
        name="t5_bias_tiles",
    )(table)


def _attn_kernel(q_ref, k_ref, v_ref, bias_ref, lam_ref, subln_ref, o_ref,
                 acc1, acc2, m1, m2, l1, l2, *, t, nkv):
    i = pl.program_id(2)
    q = q_ref[0]
    lane = lax.broadcasted_iota(jnp.int32, (1, LANES), 1)
    qa = jnp.where(lane < DA_QK_DIM, q, jnp.zeros_like(q))
    qb = jnp.where(lane >= DA_QK_DIM, q, jnp.zeros_like(q))
    for ref in (acc1, acc2, l1, l2):
        ref[...] = jnp.zeros_like(ref)
    m1[...] = jnp.full_like(m1, NEG_BIG)
    m2[...] = jnp.full_like(m2, NEG_BIG)

    def update(s, v_t, acc, m, l):
        m_old = m[...]
        m_new = jnp.maximum(m_old, jnp.max(s, axis=-1, keepdims=True))
        p = jnp.exp2(s - m_new)
        alpha = jnp.exp2(m_old - m_new)
        l[...] = alpha * l[...] + jnp.sum(p, axis=-1, keepdims=True)
        acc[...] = alpha * acc[...] + jnp.dot(p.astype(BF16), v_t, preferred_element_type=F32)
        m[...] = m_new

    def body(j, carry):
        r0 = pl.multiple_of(j * t, t)
        k_t = k_ref[0, pl.ds(r0, t), :]
        v_t = v_ref[0, pl.ds(r0, t), :]
        bias = bias_ref[0, jnp.clip(j - i, -2, 2) + 2]
        nt = (((1,), (1,)), ((), ()))
        s1 = lax.dot_general(qa, k_t, nt, preferred_element_type=F32) + bias
        s2 = lax.dot_general(qb, k_t, nt, preferred_element_type=F32) + bias
        update(s1, v_t, acc1, m1, l1)
        update(s2, v_t, acc2, m2, l2)
        return carry

    lax.fori_loop(0, nkv, body, 0)

    lv = lam_ref[...]
    lam = (jnp.exp(jnp.sum(lv[0:1] * lv[1:2], axis=-1, keepdims=True))
           - jnp.exp(jnp.sum(lv[2:3] * lv[3:4], axis=-1, keepdims=True)) + LAM_INIT)
    o = acc1[...] / l1[...] - lam * (acc2[...] / l2[...])
    o_ref[0] = (_rms(o, subln_ref[...]) * (1.0 - LAM_INIT)).astype(o_ref.dtype)


def _diff_attention(proj, bias_tiles, lam_vecs, subln, t):
    b, s, _ = proj.shape
    nkv = s // t
    return pl.pallas_call(
        functools.partial(_attn_kernel, t=t, nkv=nkv),
        grid=(b, DA_HEADS, s // t),
        in_specs=[
            pl.BlockSpec((1, t, LANES), lambda bi, h, i: (bi, i, h)),
            pl.BlockSpec((1, s, LANES), lambda bi, h, i: (bi, 0, DA_HEADS + h)),
            pl.BlockSpec((1, s, LANES), lambda bi, h, i: (bi, 0, 2 * DA_HEADS + h)),
            pl.BlockSpec((1, 5, t, t), lambda bi, h, i: (h, 0, 0, 0)),
            pl.BlockSpec((8, LANES), lambda bi, h, i: (0, 0)),
            pl.BlockSpec((1, LANES), lambda bi, h, i: (0, 0)),
        ],
        out_specs=pl.BlockSpec((1, t, LANES), lambda bi, h, i: (bi, i, h)),
        out_shape=jax.ShapeDtypeStruct((b, s, DA_HEADS * DA_V_DIM), BF16),
        scratch_shapes=[pltpu.VMEM((t, LANES), F32), pltpu.VMEM((t, LANES), F32),
                        pltpu.VMEM((t, 1), F32), pltpu.VMEM((t, 1), F32),
                        pltpu.VMEM((t, 1), F32), pltpu.VMEM((t, 1), F32)],
        compiler_params=_cparams(("parallel", "parallel", "arbitrary")),
        name="diff_attention",
    )(proj, proj, proj, bias_tiles, lam_vecs, subln)


_HALO = 16


def _conv_kernel(cur_ref, prev_ref, next_ref, w_ref, o_ref, xe_ref, *, ts):
    i = pl.program_id(1)
    n = pl.num_programs(1)
    pad = GDN_CONV // 2
    nh = 3 * GDN_HEADS
    prev_ok = jnp.where(i > 0, 1.0, 0.0)
    next_ok = jnp.where(i < n - 1, 1.0, 0.0)
    for g in range(nh):
        cols = slice(g * LANES, (g + 1) * LANES)
        xe_ref[_HALO:_HALO + ts, :] = cur_ref[0, :, cols].astype(F32)
        xe_ref[0:_HALO, :] = prev_ref[0, :, cols].astype(F32) * prev_ok
        xe_ref[_HALO + ts:2 * _HALO + ts, :] = next_ref[0, :, cols].astype(F32) * next_ok
        y = jnp.zeros((ts, LANES), F32)
        for j in range(GDN_CONV):
            y = y + w_ref[j:j + 1, cols] * xe_ref[pl.ds(_HALO - pad + j, ts), :]
        y = y * (1.0 / (1.0 + jnp.exp(-y)))
        if g < 2 * GDN_HEADS:
            y = y * lax.rsqrt(jnp.sum(y * y, axis=-1, keepdims=True) + L2_EPS)
        if g < GDN_HEADS:
            y = y * (GDN_D ** -0.5)
        o_ref[0, :, cols] = y


def _gdn_qkv(proj, conv_w, ts):
    b, s, _ = proj.shape
    width = 3 * GDN_HEADS * GDN_D
    cblk = (DA_HEADS * (2 * 2 * DA_QK_DIM + DA_V_DIM)) // width
    nb = s // _HALO
    return pl.pallas_call(
        functools.partial(_conv_kernel, ts=ts),
        grid=(b, s // ts),
        in_specs=[
            pl.BlockSpec((1, ts, width), lambda bi, i: (bi, i, cblk)),
            pl.BlockSpec((1, _HALO, width), lambda bi, i: (bi, jnp.maximum(i * (ts // _HALO) - 1, 0), cblk)),
            pl.BlockSpec((1, _HALO, width),
                         lambda bi, i: (bi, jnp.minimum((i + 1) * (ts // _HALO), nb - 1), cblk)),
            pl.BlockSpec((8, width), lambda bi, i: (0, 0)),
        ],
        out_specs=pl.BlockSpec((1, ts, width), lambda bi, i: (bi, i, 0)),
        out_shape=jax.ShapeDtypeStruct((b, s, width), F32),
        scratch_shapes=[pltpu.VMEM((ts + 2 * _HALO, LANES), F32)],
        compiler_params=_cparams(("parallel", "parallel")),
        name="gdn_conv_qkv",
    )(proj, proj, proj, conv_w)


def _bdot(a, b):
    return jnp.dot(a.astype(BF16), b.astype(BF16), preferred_element_type=F32)


def _unit_tri_inverse(lmat, r, c):
    blk16 = (r // 16) == (c // 16)
    blk32 = (r // 32) == (c // 32)
    eye = jnp.where(r == c, 1.0, 0.0).astype(F32)
    nd = jnp.where(blk16, -lmat, 0.0)
    x = eye + nd
    p = _bdot(nd, nd)
    x = x + _bdot(x, p)
    p = _bdot(p, p)
    x = x + _bdot(x, p)
    p = _bdot(p, p)
    x = x + _bdot(x, p)
    c1 = jnp.where(blk32 & (~blk16), lmat, 0.0)
    x = x - _bdot(x, _bdot(c1, x))
    c2 = jnp.where(~blk32, lmat, 0.0)
    x = x - _bdot(x, _bdot(c2, x))
    return x


def _gdn_kernel(q_ref, k_ref, v_ref, g_ref, o_ref, state_ref, *, reverse):
    @pl.when(pl.program_id(1) == 0)
    def _():
        state_ref[...] = jnp.zeros_like(state_ref)

    cz = GDN_CHUNK
    tg = 2 * cz
    g_all = g_ref[0]
    g_t = g_all.T
    r = lax.broadcasted_iota(jnp.int32, (tg, tg), 0)
    c = lax.broadcasted_iota(jnp.int32, (tg, tg), 1)
    same = (r // cz) == (c // cz)
    incl = same & ((r <= c) if reverse else (r >= c))
    strict = same & ((r < c) if reverse else (r > c))
    goff = 2 * GDN_HEADS if reverse else 0
    nt = (((1,), (1,)), ((), ()))
    tn = (((0,), (0,)), ((), ()))
    chunk_of_row = lax.broadcasted_iota(jnp.int32, (tg, 1), 0) // cz

    for h in range(GDN_HEADS):
        cols = slice(h * GDN_D, (h + 1) * GDN_D)
        q = q_ref[0, :, cols]
        k = k_ref[0, :, cols]
        v = v_ref[0, :, cols]
        gc_col = g_all[:, goff + h:goff + h + 1]
        beta = g_all[:, goff + GDN_HEADS + h:goff + GDN_HEADS + h + 1]
        gc_row = g_t[goff + h:goff + h + 1, :]
        ends = [g_all[ci * cz + (0 if reverse else cz - 1):ci * cz + (0 if reverse else cz - 1) + 1,
                      goff + h:goff + h + 1] for ci in (0, 1)]
        gl_col = jnp.where(chunk_of_row == 0, ends[0], ends[1])
        decay = jnp.where(incl, jnp.exp(jnp.where(incl, gc_col - gc_row, 0.0)), 0.0)
        kb = k * beta
        e_col = jnp.exp(gc_col)
        qk = lax.dot_general(jnp.concatenate([kb, q], axis=0).astype(BF16), k.astype(BF16), nt,
                             preferred_element_type=F32)
        lmat = jnp.where(strict, qk[:tg] * decay, 0.0)
        amat = jnp.where(incl, qk[tg:] * decay, 0.0)
        tinv = _unit_tri_inverse(lmat, r, c)
        uw = _bdot(tinv, jnp.concatenate([v * beta, kb * e_col], axis=1))
        qd = q * e_col
        k_tail = k * jnp.exp(gl_col - gc_col)
        for ci in ((1, 0) if reverse else (0, 1)):
            rows = slice(ci * cz, (ci + 1) * cz)
            state = state_ref[h]
            ws_qs = _bdot(jnp.concatenate([uw[rows, GDN_D:], qd[rows]], axis=0), state)
            v_new = uw[rows, :GDN_D] - ws_qs[:cz]
            o_ref[0, rows, cols] = ws_qs[cz:] + _bdot(amat[rows, rows], v_new)
            state_ref[h] = state * jnp.exp(ends[ci]) + lax.dot_general(
                k_tail[rows].astype(BF16), v_new.astype(BF16), tn, preferred_element_type=F32)


def _gdn_scan(qkv, gates, reverse):
    b, s, _ = qkv.shape
    tg = 2 * GDN_CHUNK
    ng = s // tg
    width = GDN_HEADS * GDN_D

    def seq(g):
        return ng - 1 - g if reverse else g

    return pl.pallas_call(
        functools.partial(_gdn_kernel, reverse=reverse),
        grid=(b, ng),
        in_specs=[
            pl.BlockSpec((1, tg, width), lambda bi, g: (bi, seq(g), 0)),
            pl.BlockSpec((1, tg, width), lambda bi, g: (bi, seq(g), 1)),
            pl.BlockSpec((1, tg, width), lambda bi, g: (bi, seq(g), 2)),
            pl.BlockSpec((1, tg, LANES), lambda bi, g: (bi, seq(g), 0)),
        ],
        out_specs=pl.BlockSpec((1, tg, width), lambda bi, g: (bi, seq(g), 0)),
        out_shape=jax.ShapeDtypeStruct((b, s, width), F32),
        scratch_shapes=[pltpu.VMEM((GDN_HEADS, GDN_D, GDN_D), F32)],
        compiler_params=_cparams(("parallel", "arbitrary")),
        name="gdn_scan_bwd" if reverse else "gdn_scan_fwd",
    )(qkv, qkv, qkv, gates)


def _outproj_kernel(x_ref, oda_ref, of_ref, ob_ref, z_ref, gn_ref, w_ref, o_ref, mix_ref):
    width = GDN_HEADS * GDN_D
    mix_ref[:, :width] = oda_ref[...]
    for h in range(GDN_HEADS):
        cols = slice(h * GDN_D, (h + 1) * GDN_D)
        og = of_ref[:, cols] + ob_ref[:, cols]
        z = z_ref[:, cols].astype(F32)
        y = _rms(og, gn_ref[...]) * (z * (1.0 / (1.0 + jnp.exp(-z))))
        mix_ref[:, width + h * GDN_D:width + (h + 1) * GDN_D] = y.astype(BF16)
    o_ref[...] = x_ref[...] + jnp.dot(mix_ref[...], w_ref[...], preferred_element_type=F32)


def _outproj(x2d, o_da, o_f, o_b, proj2d, gdn_norm, w_out, tm):
    m, d = x2d.shape
    width = GDN_HEADS * GDN_D
    zblk = proj2d.shape[1] // width - 1
    return pl.pallas_call(
        _outproj_kernel,
        grid=(m // tm,),
        in_specs=[
            pl.BlockSpec((tm, d), lambda i: (i, 0)),
            pl.BlockSpec((tm, width), lambda i: (i, 0)),
            pl.BlockSpec((tm, width), lambda i: (i, 0)),
            pl.BlockSpec((tm, width), lambda i: (i, 0)),
            pl.BlockSpec((tm, width), lambda i: (i, zblk)),
            pl.BlockSpec((1, GDN_D), lambda i: (0, 0)),
            pl.BlockSpec((2 * width, d), lambda i: (0, 0)),
        ],
        out_specs=pl.BlockSpec((tm, d), lambda i: (i, 0)),
        out_shape=jax.ShapeDtypeStruct((m, d), F32),
        scratch_shapes=[pltpu.VMEM((tm, 2 * width), BF16)],
        compiler_params=_cparams(("parallel",)),
        name="mixer_outproj",
    )(x2d, o_da, o_f, o_b, proj2d, gdn_norm.reshape(1, GDN_D), w_out)


def _cross_kernel(x_ref, nc_ref, wq_ref, k_ref, v_ref, wo_ref, nm_ref, wr_ref,
                  x2_ref, hm_ref, aff_ref, oc_ref):
    x = x_ref[...]
    hx = _rms(x, nc_ref[...]).astype(BF16)
    q = jnp.dot(hx, wq_ref[...], preferred_element_type=F32) * (CA_HEAD_DIM ** -0.5)
    nt = (((1,), (1,)), ((), ()))
    for h in range(CA_HEADS):
        cols = slice(h * CA_HEAD_DIM, (h + 1) * CA_HEAD_DIM)
        s = lax.dot_general(q[:, cols].astype(BF16), k_ref[0, :, cols], nt, preferred_element_type=F32)
        p = jnp.exp(s - jnp.max(s, axis=-1, keepdims=True))
        p = p / jnp.sum(p, axis=-1, keepdims=True)
        oc_ref[:, cols] = jnp.dot(p.astype(BF16), v_ref[0, :, cols], preferred_element_type=F32).astype(BF16)
    x2 = x + jnp.dot(oc_ref[...], wo_ref[...], preferred_element_type=F32)
    x2_ref[...] = x2
    hm = _rms(x2, nm_ref[...])
    hm_ref[...] = hm
    logits = jnp.dot(hm, wr_ref[...], preferred_element_type=F32, precision=lax.Precision.HIGHEST)
    col = lax.broadcasted_iota(jnp.int32, logits.shape, 1)
    logits = jnp.where(col < N_EXPERTS, logits, NEG_BIG)
    e = jnp.exp(logits - jnp.max(logits, axis=-1, keepdims=True))
    aff_ref[...] = e / jnp.sum(e, axis=-1, keepdims=True)


def _cross_attention(x1, norm_cross, w_cq, kv, w_co, norm_moe, w_router_pad, s, tm):
    m, d = x1.shape
    ca = CA_HEADS * CA_HEAD_DIM
    mem_len = kv.shape[1]
    per_b = s // tm
    return pl.pallas_call(
        _cross_kernel,
        grid=(m // tm,),
        in_specs=[
            pl.BlockSpec((tm, d), lambda i: (i, 0)),
            pl.BlockSpec((1, d), lambda i: (0, 0)),
            pl.BlockSpec((d, ca), lambda i: (0, 0)),
            pl.BlockSpec((1, mem_len, ca), lambda i: (i // per_b, 0, 0)),
            pl.BlockSpec((1, mem_len, ca), lambda i: (i // per_b, 0, 1)),
            pl.BlockSpec((ca, d), lambda i: (0, 0)),
            pl.BlockSpec((1, d), lambda i: (0, 0)),
            pl.BlockSpec((d, LANES), lambda i: (0, 0)),
        ],
        out_specs=[
            pl.BlockSpec((tm, d), lambda i: (i, 0)),
            pl.BlockSpec((tm, d), lambda i: (i, 0)),
            pl.BlockSpec((tm, LANES), lambda i: (i, 0)),
        ],
        out_shape=[jax.ShapeDtypeStruct((m, d), F32), jax.ShapeDtypeStruct((m, d), F32),
                   jax.ShapeDtypeStruct((m, LANES), F32)],
        scratch_shapes=[pltpu.VMEM((tm, ca), BF16)],
        compiler_params=_cparams(("parallel",)),
        name="cross_attention_router",
    )(x1, norm_cross.reshape(1, d), w_cq, kv, kv, w_co, norm_moe.reshape(1, d), w_router_pad)


def _moe_kernel(idx_ref, gate_ref, hm_hbm, xin_hbm, wg_ref, wu_ref, wd_ref, xout_hbm,
                rows_ref, xs_ref, acc_ref, sem, *, s, tc):
    del xin_hbm
    bi = pl.program_id(0)
    ct = pl.program_id(2)
    f = pl.program_id(3)
    nf = pl.num_programs(3)
    base = bi * s

    def token(r):
        return base + idx_ref[0, 0, ct * tc + r]

    def wait_rows(src_hbm):
        pltpu.make_async_copy(src_hbm.at[pl.ds(0, tc)], rows_ref, sem).wait()

    @pl.when(f == 0)
    def _():
        def issue(r, carry):
            pltpu.make_async_copy(hm_hbm.at[pl.ds(token(r), 1)], rows_ref.at[pl.ds(r, 1)], sem).start()
            return carry
        lax.fori_loop(0, tc, issue, 0)
        wait_rows(hm_hbm)
        xs_ref[...] = rows_ref[...].astype(BF16)
        acc_ref[...] = jnp.zeros_like(acc_ref)

    xs = xs_ref[...]
    a = jnp.dot(xs, wg_ref[0], preferred_element_type=F32)
    u = jnp.dot(xs, wu_ref[0], preferred_element_type=F32)
    act = (a * (1.0 / (1.0 + jnp.exp(-a))) * u).astype(BF16)
    acc_ref[...] += jnp.dot(act, wd_ref[0], preferred_element_type=F32)

    @pl.when(f == nf - 1)
    def _():
        def fetch(r, carry):
            pltpu.make_async_copy(xout_hbm.at[pl.ds(token(r), 1)], rows_ref.at[pl.ds(r, 1)], sem).start()
            return carry
        lax.fori_loop(0, tc, fetch, 0)
        wait_rows(xout_hbm)
        rows_ref[...] = rows_ref[...] + acc_ref[...] * gate_ref[0, 0]

        def put(r, carry):
            pltpu.make_async_copy(rows_ref.at[pl.ds(r, 1)], xout_hbm.at[pl.ds(token(r), 1)], sem).start()
            return carry
        lax.fori_loop(0, tc, put, 0)
        wait_rows(xout_hbm)


def _moe(idx, gate, hm, x2, w_gate, w_up, w_down, s, tc, tf):
    m, d = x2.shape
    b, e, cap = idx.shape
    ff = w_gate.shape[2]
    return pl.pallas_call(
        functools.partial(_moe_kernel, s=s, tc=tc),
        grid=(b, e, cap // tc, ff // tf),
        in_specs=[
            pl.BlockSpec((1, 1, cap), lambda bi, ei, ci, fi: (bi * e + ei, 0, 0), memory_space=pltpu.SMEM),
            pl.BlockSpec((1, 1, tc, 1), lambda bi, ei, ci, fi: (bi, ei, ci, 0)),
            pl.BlockSpec(memory_space=pl.ANY),
            pl.BlockSpec(memory_space=pl.ANY),
            pl.BlockSpec((1, d, tf), lambda bi, ei, ci, fi: (ei, 0, fi)),
            pl.BlockSpec((1, d, tf), lambda bi, ei, ci, fi: (ei, 0, fi)),
            pl.BlockSpec((1, tf, d), lambda bi, ei, ci, fi: (ei, fi, 0)),
        ],
        out_specs=pl.BlockSpec(memory_space=pl.ANY),
        out_shape=jax.ShapeDtypeStruct((m, d), F32),
        scratch_shapes=[pltpu.VMEM((tc, d), F32), pltpu.VMEM((tc, d), BF16), pltpu.VMEM((tc, d), F32),
                        pltpu.SemaphoreType.DMA(())],
        input_output_aliases={3: 0},
        compiler_params=_cparams(("arbitrary", "arbitrary", "arbitrary", "arbitrary")),
        name="moe_experts",
    )(idx.reshape(b * e, 1, cap), gate, hm, x2, w_gate, w_up, w_down)


def _final_norm_kernel(x_ref, w_ref, o_ref):
    o_ref[...] = _rms(x_ref[...], w_ref[...])


def _final_norm(x2d, w, tm):
    m, d = x2d.shape
    return pl.pallas_call(
        _final_norm_kernel,
        grid=(m // tm,),
        in_specs=[pl.BlockSpec((tm, d), lambda i: (i, 0)), pl.BlockSpec((1, d), lambda i: (0, 0))],
        out_specs=pl.BlockSpec((tm, d), lambda i: (i, 0)),
        out_shape=jax.ShapeDtypeStruct((m, d), F32),
        compiler_params=_cparams(("parallel",)),
        name="final_norm",
    )(x2d, w.reshape(1, d))


def _pick(n, pref):
    return pref if n % pref == 0 else n


def kernel(x, mem, rel_bias_table, norm_mix, w_in, conv_w, lambda_q1, lambda_k1, lambda_q2, lambda_k2,
           da_subln, gdn_a_log, gdn_dt_bias, gdn_norm, w_out, norm_cross, norm_mem, w_cq, w_ckv, w_co,
           norm_moe, w_router, w_gate, w_up, w_down, norm_final):
    b, s, d = x.shape
    m = b * s
    l = 0
    x2d = x.reshape(m, d)
    da_cols = DA_HEADS * 2 * DA_QK_DIM
    main_cols = 3 * da_cols + 4 * GDN_HEADS * GDN_D

    w_main = w_in[l, :, :main_cols].astype(BF16)
    colscale = jnp.concatenate([jnp.full((da_cols,), DA_QK_DIM ** -0.5 * LOG2E, F32),
                                jnp.ones((main_cols - da_cols,), F32)])
    proj2d = _norm_matmul(x2d, norm_mix[l], w_main, colscale, _pick(m, 1024), 1024, BF16)
    proj = proj2d.reshape(b, s, main_cols)

    w_gates = jnp.pad(w_in[l, :, main_cols:], ((0, 0), (0, LANES - 4 * GDN_HEADS)))
    zeros8 = jnp.zeros((GDN_HEADS,), F32)
    par = jnp.zeros((8, LANES), F32)
    par = par.at[0, :4 * GDN_HEADS].set(jnp.concatenate([gdn_a_log[l, 0], zeros8, gdn_a_log[l, 1], zeros8]))
    par = par.at[1, :4 * GDN_HEADS].set(jnp.concatenate([gdn_dt_bias[l, 0], zeros8, gdn_dt_bias[l, 1], zeros8]))
    gates = _gates(x2d, norm_mix[l], w_gates, par, _pick(m, 512)).reshape(b, s, LANES)

    t_att = _pick(s, ATT_TILE)
    bias_tiles = _bias_tiles(rel_bias_table.astype(F32), t_att)
    lam_vecs = jnp.zeros((8, LANES), F32)
    lam_vecs = lam_vecs.at[0, :DA_QK_DIM].set(lambda_q1[l]).at[1, :DA_QK_DIM].set(lambda_k1[l])
    lam_vecs = lam_vecs.at[2, :DA_QK_DIM].set(lambda_q2[l]).at[3, :DA_QK_DIM].set(lambda_k2[l])
    o_da = _diff_attention(proj, bias_tiles, lam_vecs, da_subln[l].reshape(1, DA_V_DIM), t_att)

    conv_pad = jnp.pad(conv_w[l], ((0, 8 - GDN_CONV), (0, 0)))
    qkv = _gdn_qkv(proj, conv_pad, _pick(s, 512))
    o_f = _gdn_scan(qkv, gates, reverse=False)
    o_b = _gdn_scan(qkv, gates, reverse=True)

    width = GDN_HEADS * GDN_D
    x1 = _outproj(x2d, o_da.reshape(m, width), o_f.reshape(m, width), o_b.reshape(m, width), proj2d,
                  gdn_norm[l], w_out[l].astype(BF16), _pick(m, 512))

    mem_len = mem.shape[1]
    ca = CA_HEADS * CA_HEAD_DIM
    kv = _norm_matmul(mem.reshape(b * mem_len, d), norm_mem[l], w_ckv[l].astype(BF16),
                      jnp.ones((2 * ca,), F32), _pick(b * mem_len, 256), _pick(2 * ca, 512), BF16)
    w_router_pad = jnp.pad(w_router[l], ((0, 0), (0, LANES - N_EXPERTS)))
    x2, hm, aff = _cross_attention(x1, norm_cross[l], w_cq[l].astype(BF16), kv.reshape(b, mem_len, 2 * ca),
                                   w_co[l].astype(BF16), norm_moe[l], w_router_pad, s, _pick(m, 512))

    cap = CAPACITY * s // N_EXPERTS
    aff_t = jnp.swapaxes(aff.reshape(b, s, LANES)[:, :, :N_EXPERTS], 1, 2)
    gate, idx = lax.top_k(aff_t, cap)
    x3 = _moe(idx.astype(jnp.int32), gate.reshape(b, N_EXPERTS, cap, 1), hm, x2,
              w_gate[l].astype(BF16), w_up[l].astype(BF16), w_down[l].astype(BF16),
              s, _pick(cap, 1024), _pick(w_gate.shape[3], 512))

    return _final_norm(x3, norm_final, _pick(m, 512)).reshape(b, s, d)
```

```python
import functools
import math

import jax
import jax.numpy as jnp
from jax import lax
from jax.experimental import pallas as pl
from jax.experimental.pallas import tpu as pltpu

F32 = jnp.float32
BF16 = jnp.bfloat16

DA_HEADS = 8
DA_QK_DIM = 64
DA_V_DIM = 128
GDN_HEADS = 8
GDN_D = 128
GDN_CONV = 5
GDN_CHUNK = 64
REL_BUCKETS = 32
CA_HEADS = 4
CA_HEAD_DIM = 128
N_EXPERTS = 16
CAPACITY = 2
NORM_EPS = 1e-6
L2_EPS = 1e-6
LAM_INIT = 0.8 - 0.6 * math.exp(-0.3 * 0)
LOG2E = math.log2(math.e)
NEG_BIG = -1e30

LANES = 128
ATT_TILE = 512
VMEM_LIMIT = 56 * 1024 * 1024


def _cparams(sem):
    return pltpu.CompilerParams(dimension_semantics=sem, vmem_limit_bytes=VMEM_LIMIT)


def _rms(x, w):
    return x * lax.rsqrt(jnp.mean(x * x, axis=-1, keepdims=True) + NORM_EPS) * w


def _norm_matmul_kernel(x_ref, nw_ref, w_ref, cs_ref, o_ref, hn_ref):
    @pl.when(pl.program_id(1) == 0)
    def _():
        hn_ref[...] = _rms(x_ref[...], nw_ref[...]).astype(BF16)

    acc = jnp.dot(hn_ref[...], w_ref[...], preferred_element_type=F32)
    o_ref[...] = (acc * cs_ref[...]).astype(o_ref.dtype)


def _norm_matmul(x2d, norm_w, w_bf16, colscale, tm, tn, out_dtype):
    m, d = x2d.shape
    n = w_bf16.shape[1]
    return pl.pallas_call(
        _norm_matmul_kernel,
        grid=(m // tm, n // tn),
        in_specs=[
            pl.BlockSpec((tm, d), lambda i, j: (i, 0)),
            pl.BlockSpec((1, d), lambda i, j: (0, 0)),
            pl.BlockSpec((d, tn), lambda i, j: (0, j)),
            pl.BlockSpec((1, tn), lambda i, j: (0, j)),
        ],
        out_specs=pl.BlockSpec((tm, tn), lambda i, j: (i, j)),
        out_shape=jax.ShapeDtypeStruct((m, n), out_dtype),
        scratch_shapes=[pltpu.VMEM((tm, d), BF16)],
        compiler_params=_cparams(("parallel", "arbitrary")),
        name="norm_matmul",
    )(x2d, norm_w.reshape(1, d), w_bf16, colscale.reshape(1, n))


def _gates_kernel(x_ref, nw_ref, w_ref, par_ref, o_ref):
    tm = x_ref.shape[0]
    hn = _rms(x_ref[...], nw_ref[...])
    pre = jnp.dot(hn, w_ref[...], preferred_element_type=F32, precision=lax.Precision.HIGHEST)
    a_log = par_ref[0:1, :]
    dt_bias = par_ref[1:2, :]
    col = lax.broadcasted_iota(jnp.int32, (1, LANES), 1)
    is_decay = (col < GDN_HEADS) | ((col >= 2 * GDN_HEADS) & (col < 3 * GDN_HEADS))
    is_bwd = col >= 2 * GDN_HEADS
    z = pre + dt_bias
    softplus = jnp.maximum(z, 0.0) + jnp.log(1.0 + jnp.exp(-jnp.abs(z)))
    log_a = -jnp.exp(a_log) * softplus
    beta = 1.0 / (1.0 + jnp.exp(-pre))
    r = lax.broadcasted_iota(jnp.int32, (tm, tm), 0)
    c = lax.broadcasted_iota(jnp.int32, (tm, tm), 1)
    same = (r // GDN_CHUNK) == (c // GDN_CHUNK)
    tri_f = jnp.where(same & (c <= r), 1.0, 0.0).astype(F32)
    tri_b = jnp.where(same & (c >= r), 1.0, 0.0).astype(F32)
    la = jnp.where(is_decay, log_a, 0.0)
    cum_f = jnp.dot(tri_f, la, preferred_element_type=F32, precision=lax.Precision.HIGHEST)
    cum_b = jnp.dot(tri_b, la, preferred_element_type=F32, precision=lax.Precision.HIGHEST)
    cum = jnp.where(is_bwd, cum_b, cum_f)
    o_ref[...] = jnp.where(is_decay, cum, beta)


def _gates(x2d, norm_w, w_gates, par, tm):
    m, d = x2d.shape
    return pl.pallas_call(
        _gates_kernel,
        grid=(m // tm,),
        in_specs=[
            pl.BlockSpec((tm, d), lambda i: (i, 0)),
            pl.BlockSpec((1, d), lambda i: (0, 0)),
            pl.BlockSpec((d, LANES), lambda i: (0, 0)),
            pl.BlockSpec((8, LANES), lambda i: (0, 0)),
        ],
        out_specs=pl.BlockSpec((tm, LANES), lambda i: (i, 0)),
        out_shape=jax.ShapeDtypeStruct((m, LANES), F32),
        compiler_params=_cparams(("parallel",)),
        name="gdn_gates",
    )(x2d, norm_w.reshape(1, d), w_gates, par)


_T5_THRESH = (12, 16, 23, 32, 46, 64, 91)


def _bias_kernel(table_ref, o_ref, *, t):
    h = pl.program_id(0)
    d = pl.program_id(1)
    r = lax.broadcasted_iota(jnp.int32, (t, 2 * t), 0)
    c = lax.broadcasted_iota(jnp.int32, (t, 2 * t), 1)
    rel = (d - 2) * t + r - jnp.where(c >= t, c - t, c)
    n = jnp.abs(rel)
    large = jnp.full((t, 2 * t), 8, jnp.int32)
    for th in _T5_THRESH:
        large = large + jnp.where(n >= th, 1, 0)
    bucket = jnp.where(rel > 0, 16, 0) + jnp.where(n < 8, n, large)
    bias = jnp.zeros((t, 2 * t), F32)
    for b in range(REL_BUCKETS):
        bias = jnp.where(bucket == b, table_ref[b, h], bias)
    o_ref[0, 0] = bias * LOG2E


def _bias_tiles(table, t):
    return pl.pallas_call(
        functools.partial(_bias_kernel, t=t),
        grid=(DA_HEADS, 5),
        in_specs=[pl.BlockSpec(memory_space=pltpu.SMEM)],
        out_specs=pl.BlockSpec((1, 1, t, 2 * t), lambda h, d: (h, d, 0, 0)),
        out_shape=jax.ShapeDtypeStruct((DA_HEADS, 5, t, 2 * t), F32),
        compiler_params=_cparams(("parallel", "parallel")),
        name="t5_bias_tiles",
    )(table)


def _attn_kernel(q_ref, k_ref, v_ref, bias_ref, lam_ref, subln_ref, o_ref, vt_ref, acc_ref,
                 s0, s1, p0, p1, a0, a1, *, t, nkv):
    i = pl.program_id(2)
    nsub = t // LANES

    @pl.when(i == 0)
    def _():
        def tr(jb, carry):
            r0 = pl.multiple_of(jb * LANES, LANES)
            vt_ref[jb] = v_ref[0, pl.ds(r0, LANES), :].astype(F32).T.astype(BF16)
            return carry
        lax.fori_loop(0, nkv * nsub, tr, 0)

    q = q_ref[0].astype(F32)
    lane = lax.broadcasted_iota(jnp.int32, (1, LANES), 1)
    qcat = jnp.concatenate([jnp.where(lane < DA_QK_DIM, q, 0.0), jnp.where(lane >= DA_QK_DIM, q, 0.0)], axis=0)
    q_t = qcat.T.astype(BF16)
    acc_ref[...] = jnp.zeros_like(acc_ref)

    def scores(j, s_ref):
        r0 = pl.multiple_of(j * t, t)
        s_ref[...] = (jnp.dot(k_ref[0, pl.ds(r0, t), :], q_t, preferred_element_type=F32)
                      + bias_ref[0, jnp.clip(j - i, -2, 2) + 2])

    def softmax(s_ref, p_ref, a_ref, ml):
        m_old, l_old = ml
        s = s_ref[...]
        m_new = jnp.maximum(m_old, jnp.max(s, axis=0, keepdims=True))
        p = jnp.exp2(s - m_new)
        alpha = jnp.exp2(m_old - m_new)
        p_ref[...] = p.astype(BF16)
        a_ref[...] = alpha
        return m_new, alpha * l_old + jnp.sum(p, axis=0, keepdims=True)

    def values(j, p_ref, a_ref):
        v_t = jnp.concatenate([vt_ref[j * nsub + u] for u in range(nsub)], axis=1)
        acc_ref[...] = a_ref[...] * acc_ref[...] + jnp.dot(v_t, p_ref[...], preferred_element_type=F32)

    ml = (jnp.full((1, 2 * t), NEG_BIG, F32), jnp.zeros((1, 2 * t), F32))
    scores(0, s0)
    scores(1, s1)
    ml = softmax(s0, p0, a0, ml)

    def body(jp, ml):
        j = 2 + 2 * jp
        scores(j, s0)
        ml = softmax(s1, p1, a1, ml)
        values(j - 2, p0, a0)
        scores(j + 1, s1)
        ml = softmax(s0, p0, a0, ml)
        values(j - 1, p1, a1)
        return ml

    ml = lax.fori_loop(0, (nkv - 2) // 2, body, ml)
    _, l_fin = softmax(s1, p1, a1, ml)
    values(nkv - 2, p0, a0)
    values(nkv - 1, p1, a1)

    lv = lam_ref[...]
    lam = (jnp.exp(jnp.sum(lv[0:1] * lv[1:2], axis=-1, keepdims=True))
           - jnp.exp(jnp.sum(lv[2:3] * lv[3:4], axis=-1, keepdims=True)) + LAM_INIT)
    on = acc_ref[...] / l_fin
    o = (on[:, :t] - lam * on[:, t:]).T
    o_ref[0] = (_rms(o, subln_ref[...]) * (1.0 - LAM_INIT)).astype(o_ref.dtype)


def _diff_attention(proj, bias_tiles, lam_vecs, subln, t):
    b, s, _ = proj.shape
    nkv = s // t
    assert nkv >= 2 and nkv % 2 == 0
    once = pl.Buffered(1)
    return pl.pallas_call(
        functools.partial(_attn_kernel, t=t, nkv=nkv),
        grid=(b, DA_HEADS, s // t),
        in_specs=[
            pl.BlockSpec((1, t, LANES), lambda bi, h, i: (bi, i, h)),
            pl.BlockSpec((1, s, LANES), lambda bi, h, i: (bi, 0, DA_HEADS + h), pipeline_mode=once),
            pl.BlockSpec((1, s, LANES), lambda bi, h, i: (bi, 0, 2 * DA_HEADS + h), pipeline_mode=once),
            pl.BlockSpec((1, 5, t, 2 * t), lambda bi, h, i: (h, 0, 0, 0), pipeline_mode=once),
            pl.BlockSpec((8, LANES), lambda bi, h, i: (0, 0)),
            pl.BlockSpec((1, LANES), lambda bi, h, i: (0, 0)),
        ],
        out_specs=pl.BlockSpec((1, t, LANES), lambda bi, h, i: (bi, i, h)),
        out_shape=jax.ShapeDtypeStruct((b, s, DA_HEADS * DA_V_DIM), BF16),
        scratch_shapes=[pltpu.VMEM((s // LANES, DA_V_DIM, LANES), BF16),
                        pltpu.VMEM((DA_V_DIM, 2 * t), F32),
                        pltpu.VMEM((t, 2 * t), F32), pltpu.VMEM((t, 2 * t), F32),
                        pltpu.VMEM((t, 2 * t), BF16), pltpu.VMEM((t, 2 * t), BF16),
                        pltpu.VMEM((1, 2 * t), F32), pltpu.VMEM((1, 2 * t), F32)],
        compiler_params=_cparams(("parallel", "parallel", "arbitrary")),
        name="diff_attention",
    )(proj, proj, proj, bias_tiles, lam_vecs, subln)


_HALO = 16


def _conv_kernel(cur_ref, prev_ref, next_ref, w_ref, o_ref, xe_ref, *, ts):
    i = pl.program_id(1)
    n = pl.num_programs(1)
    pad = GDN_CONV // 2
    nh = 3 * GDN_HEADS
    prev_ok = jnp.where(i > 0, 1.0, 0.0)
    next_ok = jnp.where(i < n - 1, 1.0, 0.0)
    for g in range(nh):
        cols = slice(g * LANES, (g + 1) * LANES)
        xe_ref[_HALO:_HALO + ts, :] = cur_ref[0, :, cols].astype(F32)
        xe_ref[0:_HALO, :] = prev_ref[0, :, cols].astype(F32) * prev_ok
        xe_ref[_HALO + ts:2 * _HALO + ts, :] = next_ref[0, :, cols].astype(F32) * next_ok
        y = jnp.zeros((ts, LANES), F32)
        for j in range(GDN_CONV):
            y = y + w_ref[j:j + 1, cols] * xe_ref[pl.ds(_HALO - pad + j, ts), :]
        y = y * (1.0 / (1.0 + jnp.exp(-y)))
        if g < 2 * GDN_HEADS:
            y = y * lax.rsqrt(jnp.sum(y * y, axis=-1, keepdims=True) + L2_EPS)
        if g < GDN_HEADS:
            y = y * (GDN_D ** -0.5)
        o_ref[0, :, cols] = y


def _gdn_qkv(proj, conv_w, ts):
    b, s, _ = proj.shape
    width = 3 * GDN_HEADS * GDN_D
    cblk = (DA_HEADS * (2 * 2 * DA_QK_DIM + DA_V_DIM)) // width
    nb = s // _HALO
    return pl.pallas_call(
        functools.partial(_conv_kernel, ts=ts),
        grid=(b, s // ts),
        in_specs=[
            pl.BlockSpec((1, ts, width), lambda bi, i: (bi, i, cblk)),
            pl.BlockSpec((1, _HALO, width), lambda bi, i: (bi, jnp.maximum(i * (ts // _HALO) - 1, 0), cblk)),
            pl.BlockSpec((1, _HALO, width),
                         lambda bi, i: (bi, jnp.minimum((i + 1) * (ts // _HALO), nb - 1), cblk)),
            pl.BlockSpec((8, width), lambda bi, i: (0, 0)),
        ],
        out_specs=pl.BlockSpec((1, ts, width), lambda bi, i: (bi, i, 0)),
        out_shape=jax.ShapeDtypeStruct((b, s, width), F32),
        scratch_shapes=[pltpu.VMEM((ts + 2 * _HALO, LANES), F32)],
        compiler_params=_cparams(("parallel", "parallel")),
        name="gdn_conv_qkv",
    )(proj, proj, proj, conv_w)


def _bdot(a, b):
    return jnp.dot(a.astype(BF16), b.astype(BF16), preferred_element_type=F32)


def _each(fn, *lists):
    return [fn(*args) for args in zip(*lists)]


def _unit_tri_inverse(lmats, r, c):
    blk16 = (r // 16) == (c // 16)
    blk32 = (r // 32) == (c // 32)
    eye = jnp.where(r == c, 1.0, 0.0).astype(F32)
    nd = [jnp.where(blk16, -m, 0.0) for m in lmats]
    x = [eye + n for n in nd]
    p = _each(_bdot, nd, nd)
    for step in range(3):
        xp = _each(_bdot, x, p)
        if step < 2:
            p = _each(_bdot, p, p)
        x = _each(jnp.add, x, xp)
    for keep in (blk32 & (~blk16), ~blk32):
        cm = [jnp.where(keep, m, 0.0) for m in lmats]
        x = _each(jnp.subtract, x, _each(_bdot, x, _each(_bdot, cm, x)))
    return x


def _gdn_kernel(q_ref, k_ref, v_ref, g_ref, o_ref, state_ref, *, reverse):
    @pl.when(pl.program_id(1) == 0)
    def _():
        state_ref[...] = jnp.zeros_like(state_ref)

    cz = GDN_CHUNK
    tg = 2 * cz
    heads = range(GDN_HEADS)
    g_all = g_ref[0]
    g_t = g_all.T
    r = lax.broadcasted_iota(jnp.int32, (tg, tg), 0)
    c = lax.broadcasted_iota(jnp.int32, (tg, tg), 1)
    same = (r // cz) == (c // cz)
    incl = same & ((r <= c) if reverse else (r >= c))
    strict = same & ((r < c) if reverse else (r > c))
    goff = 2 * GDN_HEADS if reverse else 0
    nt = (((1,), (1,)), ((), ()))
    tn = (((0,), (0,)), ((), ()))
    chunk_of_row = lax.broadcasted_iota(jnp.int32, (tg, 1), 0) // cz
    end_row = [ci * cz + (0 if reverse else cz - 1) for ci in (0, 1)]
    hcols = [slice(h * GDN_D, (h + 1) * GDN_D) for h in heads]

    q = [q_ref[0, :, cs] for cs in hcols]
    k = [k_ref[0, :, cs] for cs in hcols]
    v = [v_ref[0, :, cs] for cs in hcols]
    gc_col = [g_all[:, goff + h:goff + h + 1] for h in heads]
    beta = [g_all[:, goff + GDN_HEADS + h:goff + GDN_HEADS + h + 1] for h in heads]
    gc_row = [g_t[goff + h:goff + h + 1, :] for h in heads]
    ends = [[g_all[er:er + 1, goff + h:goff + h + 1] for er in end_row] for h in heads]
    gl_col = [jnp.where(chunk_of_row == 0, e[0], e[1]) for e in ends]
    decay = _each(lambda gc, gr: jnp.where(incl, jnp.exp(jnp.where(incl, gc - gr, 0.0)), 0.0), gc_col, gc_row)
    kb = _each(jnp.multiply, k, beta)
    e_col = [jnp.exp(gc) for gc in gc_col]
    qk = _each(lambda kb_h, q_h, k_h: lax.dot_general(
        jnp.concatenate([kb_h, q_h], axis=0).astype(BF16), k_h.astype(BF16), nt, preferred_element_type=F32),
        kb, q, k)
    lmat = _each(lambda m, d: jnp.where(strict, m[:tg] * d, 0.0), qk, decay)
    amat = _each(lambda m, d: jnp.where(incl, m[tg:] * d, 0.0), qk, decay)
    tinv = _unit_tri_inverse(lmat, r, c)
    uw = _each(lambda t_h, v_h, b_h, kb_h, e_h: _bdot(t_h, jnp.concatenate([v_h * b_h, kb_h * e_h], axis=1)),
               tinv, v, beta, kb, e_col)
    qd = _each(jnp.multiply, q, e_col)
    k_tail = _each(lambda k_h, gl, gc: k_h * jnp.exp(gl - gc), k, gl_col, gc_col)
    for ci in ((1, 0) if reverse else (0, 1)):
        rows = slice(ci * cz, (ci + 1) * cz)
        state = [state_ref[h] for h in heads]
        ws_qs = _each(lambda uw_h, qd_h, s_h: _bdot(jnp.concatenate([uw_h[rows, GDN_D:], qd_h[rows]], axis=0), s_h),
                      uw, qd, state)
        v_new = _each(lambda uw_h, x_h: uw_h[rows, :GDN_D] - x_h[:cz], uw, ws_qs)
        o = _each(lambda x_h, a_h, vn_h: x_h[cz:] + _bdot(a_h[rows, rows], vn_h), ws_qs, amat, v_new)
        new_state = _each(lambda s_h, e_h, kt_h, vn_h: s_h * jnp.exp(e_h[ci]) + lax.dot_general(
            kt_h[rows].astype(BF16), vn_h.astype(BF16), tn, preferred_element_type=F32),
            state, ends, k_tail, v_new)
        for h in heads:
            o_ref[0, rows, hcols[h]] = o[h]
            state_ref[h] = new_state[h]


def _gdn_scan(qkv, gates, reverse):
    b, s, _ = qkv.shape
    tg = 2 * GDN_CHUNK
    ng = s // tg
    width = GDN_HEADS * GDN_D

    def seq(g):
        return ng - 1 - g if reverse else g

    return pl.pallas_call(
        functools.partial(_gdn_kernel, reverse=reverse),
        grid=(b, ng),
        in_specs=[
            pl.BlockSpec((1, tg, width), lambda bi, g: (bi, seq(g), 0)),
            pl.BlockSpec((1, tg, width), lambda bi, g: (bi, seq(g), 1)),
            pl.BlockSpec((1, tg, width), lambda bi, g: (bi, seq(g), 2)),
            pl.BlockSpec((1, tg, LANES), lambda bi, g: (bi, seq(g), 0)),
        ],
        out_specs=pl.BlockSpec((1, tg, width), lambda bi, g: (bi, seq(g), 0)),
        out_shape=jax.ShapeDtypeStruct((b, s, width), F32),
        scratch_shapes=[pltpu.VMEM((GDN_HEADS, GDN_D, GDN_D), F32)],
        compiler_params=_cparams(("parallel", "arbitrary")),
        name="gdn_scan_bwd" if reverse else "gdn_scan_fwd",
    )(qkv, qkv, qkv, gates)


def _outproj_kernel(x_ref, oda_ref, of_ref, ob_ref, z_ref, gn_ref, w_ref, o_ref, mix_ref):
    width = GDN_HEADS * GDN_D
    mix_ref[:, :width] = oda_ref[...]
    for h in range(GDN_HEADS):
        cols = slice(h * GDN_D, (h + 1) * GDN_D)
        og = of_ref[:, cols] + ob_ref[:, cols]
        z = z_ref[:, cols].astype(F32)
        y = _rms(og, gn_ref[...]) * (z * (1.0 / (1.0 + jnp.exp(-z))))
        mix_ref[:, width + h * GDN_D:width + (h + 1) * GDN_D] = y.astype(BF16)
    o_ref[...] = x_ref[...] + jnp.dot(mix_ref[...], w_ref[...], preferred_element_type=F32)


def _outproj(x2d, o_da, o_f, o_b, proj2d, gdn_norm, w_out, tm):
    m, d = x2d.shape
    width = GDN_HEADS * GDN_D
    zblk = proj2d.shape[1] // width - 1
    return pl.pallas_call(
        _outproj_kernel,
        grid=(m // tm,),
        in_specs=[
            pl.BlockSpec((tm, d), lambda i: (i, 0)),
            pl.BlockSpec((tm, width), lambda i: (i, 0)),
            pl.BlockSpec((tm, width), lambda i: (i, 0)),
            pl.BlockSpec((tm, width), lambda i: (i, 0)),
            pl.BlockSpec((tm, width), lambda i: (i, zblk)),
            pl.BlockSpec((1, GDN_D), lambda i: (0, 0)),
            pl.BlockSpec((2 * width, d), lambda i: (0, 0)),
        ],
        out_specs=pl.BlockSpec((tm, d), lambda i: (i, 0)),
        out_shape=jax.ShapeDtypeStruct((m, d), F32),
        scratch_shapes=[pltpu.VMEM((tm, 2 * width), BF16)],
        compiler_params=_cparams(("parallel",)),
        name="mixer_outproj",
    )(x2d, o_da, o_f, o_b, proj2d, gdn_norm.reshape(1, GDN_D), w_out)


def _cross_kernel(x_ref, nc_ref, wq_ref, k_ref, v_ref, wo_ref, nm_ref, wr_ref,
                  x2_ref, hm_ref, aff_ref, oc_ref):
    x = x_ref[...]
    hx = _rms(x, nc_ref[...]).astype(BF16)
    q = jnp.dot(hx, wq_ref[...], preferred_element_type=F32) * (CA_HEAD_DIM ** -0.5)
    nt = (((1,), (1,)), ((), ()))
    for h in range(CA_HEADS):
        cols = slice(h * CA_HEAD_DIM, (h + 1) * CA_HEAD_DIM)
        s = lax.dot_general(q[:, cols].astype(BF16), k_ref[0, :, cols], nt, preferred_element_type=F32)
        p = jnp.exp(s - jnp.max(s, axis=-1, keepdims=True))
        p = p / jnp.sum(p, axis=-1, keepdims=True)
        oc_ref[:, cols] = jnp.dot(p.astype(BF16), v_ref[0, :, cols], preferred_element_type=F32).astype(BF16)
    x2 = x + jnp.dot(oc_ref[...], wo_ref[...], preferred_element_type=F32)
    x2_ref[...] = x2
    hm = _rms(x2, nm_ref[...])
    hm_ref[...] = hm
    logits = jnp.dot(hm, wr_ref[...], preferred_element_type=F32, precision=lax.Precision.HIGHEST)
    col = lax.broadcasted_iota(jnp.int32, logits.shape, 1)
    logits = jnp.where(col < N_EXPERTS, logits, NEG_BIG)
    e = jnp.exp(logits - jnp.max(logits, axis=-1, keepdims=True))
    aff_ref[...] = e / jnp.sum(e, axis=-1, keepdims=True)


def _cross_attention(x1, norm_cross, w_cq, kv, w_co, norm_moe, w_router_pad, s, tm):
    m, d = x1.shape
    ca = CA_HEADS * CA_HEAD_DIM
    mem_len = kv.shape[1]
    per_b = s // tm
    return pl.pallas_call(
        _cross_kernel,
        grid=(m // tm,),
        in_specs=[
            pl.BlockSpec((tm, d), lambda i: (i, 0)),
            pl.BlockSpec((1, d), lambda i: (0, 0)),
            pl.BlockSpec((d, ca), lambda i: (0, 0)),
            pl.BlockSpec((1, mem_len, ca), lambda i: (i // per_b, 0, 0)),
            pl.BlockSpec((1, mem_len, ca), lambda i: (i // per_b, 0, 1)),
            pl.BlockSpec((ca, d), lambda i: (0, 0)),
            pl.BlockSpec((1, d), lambda i: (0, 0)),
            pl.BlockSpec((d, LANES), lambda i: (0, 0)),
        ],
        out_specs=[
            pl.BlockSpec((tm, d), lambda i: (i, 0)),
            pl.BlockSpec((tm, d), lambda i: (i, 0)),
            pl.BlockSpec((tm, LANES), lambda i: (i, 0)),
        ],
        out_shape=[jax.ShapeDtypeStruct((m, d), F32), jax.ShapeDtypeStruct((m, d), F32),
                   jax.ShapeDtypeStruct((m, LANES), F32)],
        scratch_shapes=[pltpu.VMEM((tm, ca), BF16)],
        compiler_params=_cparams(("parallel",)),
        name="cross_attention_router",
    )(x1, norm_cross.reshape(1, d), w_cq, kv, kv, w_co, norm_moe.reshape(1, d), w_router_pad)


def _moe_kernel(idx_ref, gate_ref, hm_hbm, xin_hbm, wg_ref, wu_ref, wd_ref, xout_hbm,
                rows_ref, xs_ref, acc_ref, sem, *, s, tc):
    del xin_hbm
    bi = pl.program_id(0)
    ct = pl.program_id(2)
    f = pl.program_id(3)
    nf = pl.num_programs(3)
    base = bi * s

    def token(r):
        return base + idx_ref[0, 0, ct * tc + r]

    def wait_rows(src_hbm):
        pltpu.make_async_copy(src_hbm.at[pl.ds(0, tc)], rows_ref, sem).wait()

    @pl.when(f == 0)
    def _():
        def issue(r, carry):
            pltpu.make_async_copy(hm_hbm.at[pl.ds(token(r), 1)], rows_ref.at[pl.ds(r, 1)], sem).start()
            return carry
        lax.fori_loop(0, tc, issue, 0, unroll=8)
        wait_rows(hm_hbm)
        xs_ref[...] = rows_ref[...].astype(BF16)
        acc_ref[...] = jnp.zeros_like(acc_ref)

    xs = xs_ref[...]
    a = jnp.dot(xs, wg_ref[0], preferred_element_type=F32)
    u = jnp.dot(xs, wu_ref[0], preferred_element_type=F32)
    act = (a * (1.0 / (1.0 + jnp.exp(-a))) * u).astype(BF16)
    acc_ref[...] += jnp.dot(act, wd_ref[0], preferred_element_type=F32)

    @pl.when(f == nf - 1)
    def _():
        def fetch(r, carry):
            pltpu.make_async_copy(xout_hbm.at[pl.ds(token(r), 1)], rows_ref.at[pl.ds(r, 1)], sem).start()
            return carry
        lax.fori_loop(0, tc, fetch, 0, unroll=8)
        wait_rows(xout_hbm)
        rows_ref[...] = rows_ref[...] + acc_ref[...] * gate_ref[0, 0]

        def put(r, carry):
            pltpu.make_async_copy(rows_ref.at[pl.ds(r, 1)], xout_hbm.at[pl.ds(token(r), 1)], sem).start()
            return carry
        lax.fori_loop(0, tc, put, 0, unroll=8)
        wait_rows(xout_hbm)


def _moe(idx, gate, hm, x2, w_gate, w_up, w_down, s, tc, tf):
    m, d = x2.shape
    b, e, cap = idx.shape
    ff = w_gate.shape[2]
    return pl.pallas_call(
        functools.partial(_moe_kernel, s=s, tc=tc),
        grid=(b, e, cap // tc, ff // tf),
        in_specs=[
            pl.BlockSpec((1, 1, cap), lambda bi, ei, ci, fi: (bi * e + ei, 0, 0), memory_space=pltpu.SMEM),
            pl.BlockSpec((1, 1, tc, 1), lambda bi, ei, ci, fi: (bi, ei, ci, 0)),
            pl.BlockSpec(memory_space=pl.ANY),
            pl.BlockSpec(memory_space=pl.ANY),
            pl.BlockSpec((1, d, tf), lambda bi, ei, ci, fi: (ei, 0, fi)),
            pl.BlockSpec((1, d, tf), lambda bi, ei, ci, fi: (ei, 0, fi)),
            pl.BlockSpec((1, tf, d), lambda bi, ei, ci, fi: (ei, fi, 0)),
        ],
        out_specs=pl.BlockSpec(memory_space=pl.ANY),
        out_shape=jax.ShapeDtypeStruct((m, d), F32),
        scratch_shapes=[pltpu.VMEM((tc, d), F32), pltpu.VMEM((tc, d), BF16), pltpu.VMEM((tc, d), F32),
                        pltpu.SemaphoreType.DMA(())],
        input_output_aliases={3: 0},
        compiler_params=_cparams(("arbitrary", "arbitrary", "arbitrary", "arbitrary")),
        name="moe_experts",
    )(idx.reshape(b * e, 1, cap), gate, hm, x2, w_gate, w_up, w_down)


def _final_norm_kernel(x_ref, w_ref, o_ref):
    o_ref[...] = _rms(x_ref[...], w_ref[...])


def _final_norm(x2d, w, tm):
    m, d = x2d.shape
    return pl.pallas_call(
        _final_norm_kernel,
        grid=(m // tm,),
        in_specs=[pl.BlockSpec((tm, d), lambda i: (i, 0)), pl.BlockSpec((1, d), lambda i: (0, 0))],
        out_specs=pl.BlockSpec((tm, d), lambda i: (i, 0)),
        out_shape=jax.ShapeDtypeStruct((m, d), F32),
        compiler_params=_cparams(("parallel",)),
        name="final_norm",
    )(x2d, w.reshape(1, d))


def _pick(n, pref):
    return pref if n % pref == 0 else n


def kernel(x, mem, rel_bias_table, norm_mix, w_in, conv_w, lambda_q1, lambda_k1, lambda_q2, lambda_k2,
           da_subln, gdn_a_log, gdn_dt_bias, gdn_norm, w_out, norm_cross, norm_mem, w_cq, w_ckv, w_co,
           norm_moe, w_router, w_gate, w_up, w_down, norm_final):
    b, s, d = x.shape
    m = b * s
    l = 0
    x2d = x.reshape(m, d)
    da_cols = DA_HEADS * 2 * DA_QK_DIM
    main_cols = 3 * da_cols + 4 * GDN_HEADS * GDN_D

    w_main = w_in[l, :, :main_cols].astype(BF16)
    colscale = jnp.concatenate([jnp.full((da_cols,), DA_QK_DIM ** -0.5 * LOG2E, F32),
                                jnp.ones((main_cols - da_cols,), F32)])
    proj2d = _norm_matmul(x2d, norm_mix[l], w_main, colscale, _pick(m, 1024), 1024, BF16)
    proj = proj2d.reshape(b, s, main_cols)

    w_gates = jnp.pad(w_in[l, :, main_cols:], ((0, 0), (0, LANES - 4 * GDN_HEADS)))
    zeros8 = jnp.zeros((GDN_HEADS,), F32)
    par = jnp.zeros((8, LANES), F32)
    par = par.at[0, :4 * GDN_HEADS].set(jnp.concatenate([gdn_a_log[l, 0], zeros8, gdn_a_log[l, 1], zeros8]))
    par = par.at[1, :4 * GDN_HEADS].set(jnp.concatenate([gdn_dt_bias[l, 0], zeros8, gdn_dt_bias[l, 1], zeros8]))
    gates = _gates(x2d, norm_mix[l], w_gates, par, _pick(m, 512)).reshape(b, s, LANES)

    t_att = _pick(s, ATT_TILE)
    bias_tiles = _bias_tiles(rel_bias_table.astype(F32), t_att)
    lam_vecs = jnp.zeros((8, LANES), F32)
    lam_vecs = lam_vecs.at[0, :DA_QK_DIM].set(lambda_q1[l]).at[1, :DA_QK_DIM].set(lambda_k1[l])
    lam_vecs = lam_vecs.at[2, :DA_QK_DIM].set(lambda_q2[l]).at[3, :DA_QK_DIM].set(lambda_k2[l])
    o_da = _diff_attention(proj, bias_tiles, lam_vecs, da_subln[l].reshape(1, DA_V_DIM), t_att)

    conv_pad = jnp.pad(conv_w[l], ((0, 8 - GDN_CONV), (0, 0)))
    qkv = _gdn_qkv(proj, conv_pad, _pick(s, 512))
    o_f = _gdn_scan(qkv, gates, reverse=False)
    o_b = _gdn_scan(qkv, gates, reverse=True)

    width = GDN_HEADS * GDN_D
    x1 = _outproj(x2d, o_da.reshape(m, width), o_f.reshape(m, width), o_b.reshape(m, width), proj2d,
                  gdn_norm[l], w_out[l].astype(BF16), _pick(m, 512))

    mem_len = mem.shape[1]
    ca = CA_HEADS * CA_HEAD_DIM
    kv = _norm_matmul(mem.reshape(b * mem_len, d), norm_mem[l], w_ckv[l].astype(BF16),
                      jnp.ones((2 * ca,), F32), _pick(b * mem_len, 256), _pick(2 * ca, 512), BF16)
    w_router_pad = jnp.pad(w_router[l], ((0, 0), (0, LANES - N_EXPERTS)))
    x2, hm, aff = _cross_attention(x1, norm_cross[l], w_cq[l].astype(BF16), kv.reshape(b, mem_len, 2 * ca),
                                   w_co[l].astype(BF16), norm_moe[l], w_router_pad, s, _pick(m, 512))

    cap = CAPACITY * s // N_EXPERTS
    aff_t = jnp.swapaxes(aff.reshape(b, s, LANES)[:, :, :N_EXPERTS], 1, 2)
    gate, idx = lax.top_k(aff_t, cap)
    x3 = _moe(idx.astype(jnp.int32), gate.reshape(b, N_EXPERTS, cap, 1), hm, x2,
              w_gate[l].astype(BF16), w_up[l].astype(BF16), w_down[l].astype(BF16),
              s, _pick(cap, 1024), _pick(w_gate.shape[3], 512))

    return _final_norm(x3, norm_final, _pick(m, 512)).reshape(b, s, d)
```

```python
import functools
import math

import jax
import jax.numpy as jnp
from jax import lax
from jax.experimental import pallas as pl
from jax.experimental.pallas import tpu as pltpu

F32 = jnp.float32
BF16 = jnp.bfloat16

DA_HEADS = 8
DA_QK_DIM = 64
DA_V_DIM = 128
GDN_HEADS = 8
GDN_D = 128
GDN_CONV = 5
GDN_CHUNK = 64
REL_BUCKETS = 32
CA_HEADS = 4
CA_HEAD_DIM = 128
N_EXPERTS = 16
CAPACITY = 2
NORM_EPS = 1e-6
L2_EPS = 1e-6
LAM_INIT = 0.8 - 0.6 * math.exp(-0.3 * 0)
LOG2E = math.log2(math.e)
NEG_BIG = -1e30

LANES = 128
ATT_TILE = 512
ONES_ROWS = 16
VMEM_LIMIT = 56 * 1024 * 1024


def _cparams(sem, flags=None):
    return pltpu.CompilerParams(dimension_semantics=sem, vmem_limit_bytes=VMEM_LIMIT, flags=flags)


def _rms(x, w):
    return x * lax.rsqrt(jnp.mean(x * x, axis=-1, keepdims=True) + NORM_EPS) * w


def _split_bf16(x, parts):
    out = []
    for _ in range(parts):
        piece = x.astype(BF16)
        out.append(piece)
        x = x - piece.astype(F32)
    return out


def _dot_split(a, b):
    a_hi, a_lo = _split_bf16(a, 2)
    b_hi, b_lo = _split_bf16(b, 2)
    return (jnp.dot(a_hi, b_hi, preferred_element_type=F32) + jnp.dot(a_hi, b_lo, preferred_element_type=F32)
            + jnp.dot(a_lo, b_hi, preferred_element_type=F32))


def _dot_exact_lhs(a_bf16, b):
    return sum(jnp.dot(a_bf16, piece, preferred_element_type=F32) for piece in _split_bf16(b, 3))


def _norm_matmul_kernel(x_ref, nw_ref, w_ref, cs_ref, o_ref, hn_ref):
    @pl.when(pl.program_id(1) == 0)
    def _():
        hn_ref[...] = _rms(x_ref[...], nw_ref[...]).astype(BF16)

    acc = jnp.dot(hn_ref[...], w_ref[...], preferred_element_type=F32)
    o_ref[...] = (acc * cs_ref[...]).astype(o_ref.dtype)


def _norm_matmul(x2d, norm_w, w_bf16, colscale, tm, tn, out_dtype):
    m, d = x2d.shape
    n = w_bf16.shape[1]
    return pl.pallas_call(
        _norm_matmul_kernel,
        grid=(m // tm, n // tn),
        in_specs=[
            pl.BlockSpec((tm, d), lambda i, j: (i, 0)),
            pl.BlockSpec((1, d), lambda i, j: (0, 0)),
            pl.BlockSpec((d, tn), lambda i, j: (0, j)),
            pl.BlockSpec((1, tn), lambda i, j: (0, j)),
        ],
        out_specs=pl.BlockSpec((tm, tn), lambda i, j: (i, j)),
        out_shape=jax.ShapeDtypeStruct((m, n), out_dtype),
        scratch_shapes=[pltpu.VMEM((tm, d), BF16)],
        compiler_params=_cparams(("parallel", "arbitrary")),
        name="norm_matmul",
    )(x2d, norm_w.reshape(1, d), w_bf16, colscale.reshape(1, n))


def _gates_kernel(x_ref, nw_ref, w_ref, par_ref, o_ref):
    tm = x_ref.shape[0]
    hn = _rms(x_ref[...], nw_ref[...])
    pre = _dot_split(hn, w_ref[...])
    a_log = par_ref[0:1, :]
    dt_bias = par_ref[1:2, :]
    col = lax.broadcasted_iota(jnp.int32, (1, LANES), 1)
    is_decay = (col < GDN_HEADS) | ((col >= 2 * GDN_HEADS) & (col < 3 * GDN_HEADS))
    is_bwd = col >= 2 * GDN_HEADS
    z = pre + dt_bias
    softplus = jnp.maximum(z, 0.0) + jnp.log(1.0 + jnp.exp(-jnp.abs(z)))
    log_a = -jnp.exp(a_log) * softplus
    beta = 1.0 / (1.0 + jnp.exp(-pre))
    r = lax.broadcasted_iota(jnp.int32, (tm, tm), 0)
    c = lax.broadcasted_iota(jnp.int32, (tm, tm), 1)
    same = (r // GDN_CHUNK) == (c // GDN_CHUNK)
    tri_f = jnp.where(same & (c <= r), 1.0, 0.0).astype(BF16)
    tri_b = jnp.where(same & (c >= r), 1.0, 0.0).astype(BF16)
    la = jnp.where(is_decay, log_a, 0.0)
    cum_f = _dot_exact_lhs(tri_f, la)
    cum_b = _dot_exact_lhs(tri_b, la)
    cum = jnp.where(is_bwd, cum_b, cum_f)
    o_ref[...] = jnp.where(is_decay, cum, beta)


def _gates(x2d, norm_w, w_gates, par, tm):
    m, d = x2d.shape
    return pl.pallas_call(
        _gates_kernel,
        grid=(m // tm,),
        in_specs=[
            pl.BlockSpec((tm, d), lambda i: (i, 0)),
            pl.BlockSpec((1, d), lambda i: (0, 0)),
            pl.BlockSpec((d, LANES), lambda i: (0, 0)),
            pl.BlockSpec((8, LANES), lambda i: (0, 0)),
        ],
        out_specs=pl.BlockSpec((tm, LANES), lambda i: (i, 0)),
        out_shape=jax.ShapeDtypeStruct((m, LANES), F32),
        compiler_params=_cparams(("parallel",)),
        name="gdn_gates",
    )(x2d, norm_w.reshape(1, d), w_gates, par)


_T5_THRESH = (12, 16, 23, 32, 46, 64, 91)


def _bias_kernel(table_ref, o_ref, *, t):
    h = pl.program_id(0)
    d = pl.program_id(1)
    r = lax.broadcasted_iota(jnp.int32, (t, t), 0)
    c = lax.broadcasted_iota(jnp.int32, (t, t), 1)
    rel = (d - 2) * t + r - c
    n = jnp.abs(rel)
    large = jnp.full((t, t), 8, jnp.int32)
    for th in _T5_THRESH:
        large = large + jnp.where(n >= th, 1, 0)
    bucket = jnp.where(rel > 0, 16, 0) + jnp.where(n < 8, n, large)
    bias = jnp.zeros((t, t), F32)
    for b in range(REL_BUCKETS):
        bias = jnp.where(bucket == b, table_ref[b, h], bias)
    o_ref[0, 0, :, :t] = bias * LOG2E
    o_ref[0, 0, :, t:] = bias * LOG2E


def _bias_tiles(table, t):
    return pl.pallas_call(
        functools.partial(_bias_kernel, t=t),
        grid=(DA_HEADS, 5),
        in_specs=[pl.BlockSpec(memory_space=pltpu.SMEM)],
        out_specs=pl.BlockSpec((1, 1, t, 2 * t), lambda h, d: (h, d, 0, 0)),
        out_shape=jax.ShapeDtypeStruct((DA_HEADS, 5, t, 2 * t), F32),
        compiler_params=_cparams(("parallel", "parallel")),
        name="t5_bias_tiles",
    )(table)


def _attn_kernel(table_ref, q_ref, k_ref, v_ref, bias_ref, lam_ref, subln_ref, o_ref, vt_ref, acc_ref,
                 s0, s1, p0, p1, a0, a1, *, t, nkv):
    i = pl.program_id(2)
    nsub = t // LANES

    @pl.when(i == 0)
    def _():
        def tr(jb, carry):
            r0 = pl.multiple_of(jb * LANES, LANES)
            vt_ref[jb] = v_ref[0, pl.ds(r0, LANES), :].astype(F32).T.astype(BF16)
            return carry
        lax.fori_loop(0, nkv * nsub, tr, 0)

    q = q_ref[0].astype(F32)
    lane = lax.broadcasted_iota(jnp.int32, (1, LANES), 1)
    qcat = jnp.concatenate([jnp.where(lane < DA_QK_DIM, q, 0.0), jnp.where(lane >= DA_QK_DIM, q, 0.0)], axis=0)
    q_t = qcat.T.astype(BF16)
    acc_ref[...] = jnp.zeros_like(acc_ref)

    h = pl.program_id(1)
    base = jnp.clip(i - 1, 0, nkv - 3)
    c_left = table_ref[REL_BUCKETS // 2 - 1, h] * LOG2E
    c_right = table_ref[REL_BUCKETS - 1, h] * LOG2E

    def tile(step):
        f = step - 3
        return jnp.where(step < 3, base + step, f + jnp.where(f >= base, 3, 0))

    def far_bias(step):
        return jnp.where(tile(step) < i, c_left, c_right)

    cw = 2 * LANES
    chunks = [slice(u * cw, (u + 1) * cw) for u in range(2 * t // cw)]

    def pipe_step(ml, sc=None, sm=None, va=None):
        if sc is not None:
            step, s_dst, near = sc
            j = tile(step)
            k_t = k_ref[0, pl.ds(pl.multiple_of(j * t, t), t), :]
        if sm is not None:
            s_src, p_dst, a_dst, c = sm
        if va is not None:
            step_v, p_src, a_src = va
            jv = tile(step_v)
            v_t = jnp.concatenate([jnp.concatenate([vt_ref[jv * nsub + u] for u in range(nsub)], axis=1),
                                   jnp.ones((ONES_ROWS, t), BF16)], axis=0)
        out = []
        for u, cc in enumerate(chunks):
            if sc is not None:
                s = jnp.dot(k_t, q_t[:, cc], preferred_element_type=F32)
                s_dst[:, cc] = s + bias_ref[0, j - i + 2, :, cc] if near else s
            if sm is not None:
                m_old = ml[u]
                s = s_src[:, cc]
                m_new = jnp.maximum(m_old, jnp.max(s, axis=0, keepdims=True) + c)
                p_dst[:, cc] = jnp.exp2((s - (m_new - c)).astype(BF16))
                a_dst[:, cc] = jnp.exp2(m_old - m_new)
                out.append(m_new)
            if va is not None:
                acc_ref[:, cc] = (a_src[:, cc] * acc_ref[:, cc]
                                  + jnp.dot(v_t, p_src[:, cc], preferred_element_type=F32))
        return out if sm is not None else ml

    ml = [jnp.full((1, cw), NEG_BIG, F32) for _ in chunks]
    ml = pipe_step(ml, sc=(0, s0, True))
    ml = pipe_step(ml, sc=(1, s1, True), sm=(s0, p0, a0, 0.0))
    ml = pipe_step(ml, sc=(2, s0, True), sm=(s1, p1, a1, 0.0), va=(0, p0, a0))
    ml = pipe_step(ml, sc=(3, s1, False), sm=(s0, p0, a0, 0.0), va=(1, p1, a1))

    def body(sp, ml):
        step = 4 + 2 * sp
        ml = pipe_step(ml, sc=(step, s0, False), sm=(s1, p1, a1, far_bias(step - 1)), va=(step - 2, p0, a0))
        ml = pipe_step(ml, sc=(step + 1, s1, False), sm=(s0, p0, a0, far_bias(step)), va=(step - 1, p1, a1))
        return ml

    ml = lax.fori_loop(0, (nkv - 4) // 2, body, ml)
    ml = pipe_step(ml, sm=(s1, p1, a1, far_bias(nkv - 1)), va=(nkv - 2, p0, a0))
    pipe_step(ml, va=(nkv - 1, p1, a1))

    lv = lam_ref[...]
    lam = (jnp.exp(jnp.sum(lv[0:1] * lv[1:2], axis=-1, keepdims=True))
           - jnp.exp(jnp.sum(lv[2:3] * lv[3:4], axis=-1, keepdims=True)) + LAM_INIT)
    on = acc_ref[:DA_V_DIM, :] / acc_ref[DA_V_DIM:DA_V_DIM + 1, :]
    o = (on[:, :t] - lam * on[:, t:]).T
    o_ref[0] = (_rms(o, subln_ref[...]) * (1.0 - LAM_INIT)).astype(o_ref.dtype)


def _diff_attention(table, proj, bias_tiles, lam_vecs, subln, t):
    b, s, _ = proj.shape
    nkv = s // t
    assert nkv >= 4 and nkv % 2 == 0 and t >= _T5_THRESH[-1]
    once = pl.Buffered(1)
    return pl.pallas_call(
        functools.partial(_attn_kernel, t=t, nkv=nkv),
        grid=(b, DA_HEADS, s // t),
        in_specs=[
            pl.BlockSpec(memory_space=pltpu.SMEM),
            pl.BlockSpec((1, t, LANES), lambda bi, h, i: (bi, i, h)),
            pl.BlockSpec((1, s, LANES), lambda bi, h, i: (bi, 0, DA_HEADS + h), pipeline_mode=once),
            pl.BlockSpec((1, s, LANES), lambda bi, h, i: (bi, 0, 2 * DA_HEADS + h), pipeline_mode=once),
            pl.BlockSpec((1, 5, t, 2 * t), lambda bi, h, i: (h, 0, 0, 0), pipeline_mode=once),
            pl.BlockSpec((8, LANES), lambda bi, h, i: (0, 0)),
            pl.BlockSpec((1, LANES), lambda bi, h, i: (0, 0)),
        ],
        out_specs=pl.BlockSpec((1, t, LANES), lambda bi, h, i: (bi, i, h)),
        out_shape=jax.ShapeDtypeStruct((b, s, DA_HEADS * DA_V_DIM), BF16),
        scratch_shapes=[pltpu.VMEM((s // LANES, DA_V_DIM, LANES), BF16),
                        pltpu.VMEM((DA_V_DIM + ONES_ROWS, 2 * t), F32),
                        pltpu.VMEM((t, 2 * t), F32), pltpu.VMEM((t, 2 * t), F32),
                        pltpu.VMEM((t, 2 * t), BF16), pltpu.VMEM((t, 2 * t), BF16),
                        pltpu.VMEM((1, 2 * t), F32), pltpu.VMEM((1, 2 * t), F32)],
        compiler_params=_cparams(("parallel", "parallel", "arbitrary")),
        name="diff_attention",
    )(table, proj, proj, proj, bias_tiles, lam_vecs, subln)


_HALO = 16


def _conv_kernel(cur_ref, prev_ref, next_ref, w_ref, o_ref, xe_ref, *, ts):
    i = pl.program_id(1)
    n = pl.num_programs(1)
    pad = GDN_CONV // 2
    nh = 3 * GDN_HEADS
    prev_ok = jnp.where(i > 0, 1.0, 0.0)
    next_ok = jnp.where(i < n - 1, 1.0, 0.0)
    for g in range(nh):
        cols = slice(g * LANES, (g + 1) * LANES)
        xe_ref[_HALO:_HALO + ts, :] = cur_ref[0, :, cols].astype(F32)
        xe_ref[0:_HALO, :] = prev_ref[0, :, cols].astype(F32) * prev_ok
        xe_ref[_HALO + ts:2 * _HALO + ts, :] = next_ref[0, :, cols].astype(F32) * next_ok
        y = jnp.zeros((ts, LANES), F32)
        for j in range(GDN_CONV):
            y = y + w_ref[j:j + 1, cols] * xe_ref[pl.ds(_HALO - pad + j, ts), :]
        y = y * (1.0 / (1.0 + jnp.exp(-y)))
        if g < 2 * GDN_HEADS:
            y = y * lax.rsqrt(jnp.sum(y * y, axis=-1, keepdims=True) + L2_EPS)
        if g < GDN_HEADS:
            y = y * (GDN_D ** -0.5)
        o_ref[0, :, cols] = y


def _gdn_qkv(proj, conv_w, ts):
    b, s, _ = proj.shape
    width = 3 * GDN_HEADS * GDN_D
    cblk = (DA_HEADS * (2 * 2 * DA_QK_DIM + DA_V_DIM)) // width
    nb = s // _HALO
    return pl.pallas_call(
        functools.partial(_conv_kernel, ts=ts),
        grid=(b, s // ts),
        in_specs=[
            pl.BlockSpec((1, ts, width), lambda bi, i: (bi, i, cblk)),
            pl.BlockSpec((1, _HALO, width), lambda bi, i: (bi, jnp.maximum(i * (ts // _HALO) - 1, 0), cblk)),
            pl.BlockSpec((1, _HALO, width),
                         lambda bi, i: (bi, jnp.minimum((i + 1) * (ts // _HALO), nb - 1), cblk)),
            pl.BlockSpec((8, width), lambda bi, i: (0, 0)),
        ],
        out_specs=pl.BlockSpec((1, ts, width), lambda bi, i: (bi, i, 0)),
        out_shape=jax.ShapeDtypeStruct((b, s, width), F32),
        scratch_shapes=[pltpu.VMEM((ts + 2 * _HALO, LANES), F32)],
        compiler_params=_cparams(("parallel", "parallel")),
        name="gdn_conv_qkv",
    )(proj, proj, proj, conv_w)


def _bdot(a, b):
    return jnp.dot(a.astype(BF16), b.astype(BF16), preferred_element_type=F32)


def _each(fn, *lists):
    return [fn(*args) for args in zip(*lists)]


def _unit_tri_inverse(lmats, r, c):
    blk16 = (r // 16) == (c // 16)
    blk32 = (r // 32) == (c // 32)
    eye = jnp.where(r == c, 1.0, 0.0).astype(F32)
    nd = [jnp.where(blk16, -m, 0.0) for m in lmats]
    x = [eye + n for n in nd]
    p = _each(_bdot, nd, nd)
    for step in range(3):
        xp = _each(_bdot, x, p)
        if step < 2:
            p = _each(_bdot, p, p)
        x = _each(jnp.add, x, xp)
    for keep in (blk32 & (~blk16), ~blk32):
        cm = [jnp.where(keep, m, 0.0) for m in lmats]
        x = _each(jnp.subtract, x, _each(_bdot, x, _each(_bdot, cm, x)))
    return x


def _gdn_kernel(q_ref, k_ref, v_ref, g_ref, o_ref, state_ref, *, reverse):
    @pl.when(pl.program_id(1) == 0)
    def _():
        state_ref[...] = jnp.zeros_like(state_ref)

    cz = GDN_CHUNK
    tg = 2 * cz
    heads = range(GDN_HEADS)
    g_all = g_ref[0]
    g_t = g_all.T
    r = lax.broadcasted_iota(jnp.int32, (tg, tg), 0)
    c = lax.broadcasted_iota(jnp.int32, (tg, tg), 1)
    same = (r // cz) == (c // cz)
    incl = same & ((r <= c) if reverse else (r >= c))
    strict = same & ((r < c) if reverse else (r > c))
    goff = 2 * GDN_HEADS if reverse else 0
    nt = (((1,), (1,)), ((), ()))
    tn = (((0,), (0,)), ((), ()))
    chunk_of_row = lax.broadcasted_iota(jnp.int32, (tg, 1), 0) // cz
    end_row = [ci * cz + (0 if reverse else cz - 1) for ci in (0, 1)]
    hcols = [slice(h * GDN_D, (h + 1) * GDN_D) for h in heads]

    q = [q_ref[0, :, cs] for cs in hcols]
    k = [k_ref[0, :, cs] for cs in hcols]
    v = [v_ref[0, :, cs] for cs in hcols]
    gc_col = [g_all[:, goff + h:goff + h + 1] for h in heads]
    beta = [g_all[:, goff + GDN_HEADS + h:goff + GDN_HEADS + h + 1] for h in heads]
    gc_row = [g_t[goff + h:goff + h + 1, :] for h in heads]
    ends = [[g_all[er:er + 1, goff + h:goff + h + 1] for er in end_row] for h in heads]
    gl_col = [jnp.where(chunk_of_row == 0, e[0], e[1]) for e in ends]
    decay = _each(lambda gc, gr: jnp.where(incl, jnp.exp(jnp.where(incl, gc - gr, 0.0)), 0.0), gc_col, gc_row)
    kb = _each(jnp.multiply, k, beta)
    e_col = [jnp.exp(gc) for gc in gc_col]
    qk = _each(lambda kb_h, q_h, k_h: lax.dot_general(
        jnp.concatenate([kb_h, q_h], axis=0).astype(BF16), k_h.astype(BF16), nt, preferred_element_type=F32),
        kb, q, k)
    lmat = _each(lambda m, d: jnp.where(strict, m[:tg] * d, 0.0), qk, decay)
    amat = _each(lambda m, d: jnp.where(incl, m[tg:] * d, 0.0), qk, decay)
    tinv = _unit_tri_inverse(lmat, r, c)
    uw = _each(lambda t_h, v_h, b_h, kb_h, e_h: _bdot(t_h, jnp.concatenate([v_h * b_h, kb_h * e_h], axis=1)),
               tinv, v, beta, kb, e_col)
    qd = _each(jnp.multiply, q, e_col)
    k_tail = _each(lambda k_h, gl, gc: k_h * jnp.exp(gl - gc), k, gl_col, gc_col)
    for ci in ((1, 0) if reverse else (0, 1)):
        rows = slice(ci * cz, (ci + 1) * cz)
        state = [state_ref[h] for h in heads]
        ws_qs = _each(lambda uw_h, qd_h, s_h: _bdot(jnp.concatenate([uw_h[rows, GDN_D:], qd_h[rows]], axis=0), s_h),
                      uw, qd, state)
        v_new = _each(lambda uw_h, x_h: uw_h[rows, :GDN_D] - x_h[:cz], uw, ws_qs)
        o = _each(lambda x_h, a_h, vn_h: x_h[cz:] + _bdot(a_h[rows, rows], vn_h), ws_qs, amat, v_new)
        new_state = _each(lambda s_h, e_h, kt_h, vn_h: s_h * jnp.exp(e_h[ci]) + lax.dot_general(
            kt_h[rows].astype(BF16), vn_h.astype(BF16), tn, preferred_element_type=F32),
            state, ends, k_tail, v_new)
        for h in heads:
            o_ref[0, rows, hcols[h]] = o[h]
            state_ref[h] = new_state[h]


def _gdn_scan(qkv, gates, reverse):
    b, s, _ = qkv.shape
    tg = 2 * GDN_CHUNK
    ng = s // tg
    width = GDN_HEADS * GDN_D

    def seq(g):
        return ng - 1 - g if reverse else g

    return pl.pallas_call(
        functools.partial(_gdn_kernel, reverse=reverse),
        grid=(b, ng),
        in_specs=[
            pl.BlockSpec((1, tg, width), lambda bi, g: (bi, seq(g), 0)),
            pl.BlockSpec((1, tg, width), lambda bi, g: (bi, seq(g), 1)),
            pl.BlockSpec((1, tg, width), lambda bi, g: (bi, seq(g), 2)),
            pl.BlockSpec((1, tg, LANES), lambda bi, g: (bi, seq(g), 0)),
        ],
        out_specs=pl.BlockSpec((1, tg, width), lambda bi, g: (bi, seq(g), 0)),
        out_shape=jax.ShapeDtypeStruct((b, s, width), F32),
        scratch_shapes=[pltpu.VMEM((GDN_HEADS, GDN_D, GDN_D), F32)],
        compiler_params=_cparams(("parallel", "arbitrary")),
        name="gdn_scan_bwd" if reverse else "gdn_scan_fwd",
    )(qkv, qkv, qkv, gates)


def _outproj_kernel(x_ref, oda_ref, of_ref, ob_ref, z_ref, gn_ref, w_ref, o_ref, mix_ref):
    width = GDN_HEADS * GDN_D
    mix_ref[:, :width] = oda_ref[...]
    for h in range(GDN_HEADS):
        cols = slice(h * GDN_D, (h + 1) * GDN_D)
        og = of_ref[:, cols] + ob_ref[:, cols]
        z = z_ref[:, cols].astype(F32)
        y = _rms(og, gn_ref[...]) * (z * (1.0 / (1.0 + jnp.exp(-z))))
        mix_ref[:, width + h * GDN_D:width + (h + 1) * GDN_D] = y.astype(BF16)
    o_ref[...] = x_ref[...] + jnp.dot(mix_ref[...], w_ref[...], preferred_element_type=F32)


def _outproj(x2d, o_da, o_f, o_b, proj2d, gdn_norm, w_out, tm):
    m, d = x2d.shape
    width = GDN_HEADS * GDN_D
    zblk = proj2d.shape[1] // width - 1
    return pl.pallas_call(
        _outproj_kernel,
        grid=(m // tm,),
        in_specs=[
            pl.BlockSpec((tm, d), lambda i: (i, 0)),
            pl.BlockSpec((tm, width), lambda i: (i, 0)),
            pl.BlockSpec((tm, width), lambda i: (i, 0)),
            pl.BlockSpec((tm, width), lambda i: (i, 0)),
            pl.BlockSpec((tm, width), lambda i: (i, zblk)),
            pl.BlockSpec((1, GDN_D), lambda i: (0, 0)),
            pl.BlockSpec((2 * width, d), lambda i: (0, 0)),
        ],
        out_specs=pl.BlockSpec((tm, d), lambda i: (i, 0)),
        out_shape=jax.ShapeDtypeStruct((m, d), F32),
        scratch_shapes=[pltpu.VMEM((tm, 2 * width), BF16)],
        compiler_params=_cparams(("parallel",)),
        name="mixer_outproj",
    )(x2d, o_da, o_f, o_b, proj2d, gdn_norm.reshape(1, GDN_D), w_out)


def _cross_kernel(x_ref, nc_ref, wq_ref, k_ref, v_ref, wo_ref, nm_ref, wr_ref,
                  x2_ref, hm_ref, aff_ref, oc_ref):
    x = x_ref[...]
    hx = _rms(x, nc_ref[...]).astype(BF16)
    q = jnp.dot(hx, wq_ref[...], preferred_element_type=F32) * (CA_HEAD_DIM ** -0.5)
    nt = (((1,), (1,)), ((), ()))
    for h in range(CA_HEADS):
        cols = slice(h * CA_HEAD_DIM, (h + 1) * CA_HEAD_DIM)
        s = lax.dot_general(q[:, cols].astype(BF16), k_ref[0, :, cols], nt, preferred_element_type=F32)
        p = jnp.exp(s - jnp.max(s, axis=-1, keepdims=True))
        p = p / jnp.sum(p, axis=-1, keepdims=True)
        oc_ref[:, cols] = jnp.dot(p.astype(BF16), v_ref[0, :, cols], preferred_element_type=F32).astype(BF16)
    x2 = x + jnp.dot(oc_ref[...], wo_ref[...], preferred_element_type=F32)
    x2_ref[...] = x2
    hm = _rms(x2, nm_ref[...])
    hm_ref[...] = hm
    logits = _dot_split(hm, wr_ref[...])
    col = lax.broadcasted_iota(jnp.int32, logits.shape, 1)
    logits = jnp.where(col < N_EXPERTS, logits, NEG_BIG)
    e = jnp.exp(logits - jnp.max(logits, axis=-1, keepdims=True))
    aff_ref[...] = e / jnp.sum(e, axis=-1, keepdims=True)


def _cross_attention(x1, norm_cross, w_cq, kv, w_co, norm_moe, w_router_pad, s, tm):
    m, d = x1.shape
    ca = CA_HEADS * CA_HEAD_DIM
    mem_len = kv.shape[1]
    per_b = s // tm
    return pl.pallas_call(
        _cross_kernel,
        grid=(m // tm,),
        in_specs=[
            pl.BlockSpec((tm, d), lambda i: (i, 0)),
            pl.BlockSpec((1, d), lambda i: (0, 0)),
            pl.BlockSpec((d, ca), lambda i: (0, 0)),
            pl.BlockSpec((1, mem_len, ca), lambda i: (i // per_b, 0, 0)),
            pl.BlockSpec((1, mem_len, ca), lambda i: (i // per_b, 0, 1)),
            pl.BlockSpec((ca, d), lambda i: (0, 0)),
            pl.BlockSpec((1, d), lambda i: (0, 0)),
            pl.BlockSpec((d, LANES), lambda i: (0, 0)),
        ],
        out_specs=[
            pl.BlockSpec((tm, d), lambda i: (i, 0)),
            pl.BlockSpec((tm, d), lambda i: (i, 0)),
            pl.BlockSpec((tm, LANES), lambda i: (i, 0)),
        ],
        out_shape=[jax.ShapeDtypeStruct((m, d), F32), jax.ShapeDtypeStruct((m, d), F32),
                   jax.ShapeDtypeStruct((m, LANES), F32)],
        scratch_shapes=[pltpu.VMEM((tm, ca), BF16)],
        compiler_params=_cparams(("parallel",)),
        name="cross_attention_router",
    )(x1, norm_cross.reshape(1, d), w_cq, kv, kv, w_co, norm_moe.reshape(1, d), w_router_pad)


def _route_select_kernel(aff_ref, pos_ref, rowstart_ref, *, cap):
    ne, nr = aff_ref.shape[1], aff_ref.shape[2]
    bits = pltpu.bitcast(aff_ref[0], jnp.int32)

    def count(mask):
        ones = jnp.where(mask, 1.0, 0.0)
        return jnp.sum(jnp.sum(ones, axis=1, keepdims=True), axis=2, keepdims=True)

    def bisect(step, thr):
        cand = thr | (jnp.int32(1) << (30 - step))
        return jnp.where(count(bits >= cand) >= cap, cand, thr)

    thr = lax.fori_loop(0, 31, bisect, jnp.zeros((ne, 1, 1), jnp.int32))
    gt = bits > thr
    eq = bits == thr
    need = cap - count(gt)

    ri = lax.broadcasted_iota(jnp.int32, (LANES, LANES), 0)
    ci = lax.broadcasted_iota(jnp.int32, (LANES, LANES), 1)
    lane_before = jnp.where(ri < ci, 1.0, 0.0).astype(BF16)
    rr = lax.broadcasted_iota(jnp.int32, (nr, nr), 0)
    rc = lax.broadcasted_iota(jnp.int32, (nr, nr), 1)
    row_before = jnp.where(rc < rr, 1.0, 0.0).astype(BF16)
    row_after = jnp.where(rr < rc, 1.0, 0.0).astype(BF16)
    ones_rows = jnp.ones((8, LANES), BF16)
    nt = (((1,), (1,)), ((), ()))

    def prefix(x):
        in_row = jnp.dot(x, lane_before, preferred_element_type=F32)
        rows_above = jnp.sum(jnp.dot(row_before, x, preferred_element_type=F32), axis=1, keepdims=True)
        return in_row + rows_above

    for e in range(ne):
        eq_e = jnp.where(eq[e], 1.0, 0.0).astype(BF16)
        keep_eq = eq[e] & (prefix(eq_e) < need[e])
        sel = gt[e] | keep_eq
        sel_b = jnp.where(sel, 1.0, 0.0).astype(BF16)
        pos_ref[0, e] = jnp.where(sel, prefix(sel_b), -1.0)
        row_tot = lax.dot_general(ones_rows, sel_b, nt, preferred_element_type=F32)
        rowstart_ref[0, e] = jnp.dot(row_tot.astype(BF16), row_after, preferred_element_type=F32).astype(jnp.int32)


def _route_compact_kernel(rowstart_ref, pos_ref, aff_ref, o_ref, *, cap):
    nr = pos_ref.shape[2]
    win = 2 * LANES
    o_ref[...] = jnp.zeros_like(o_ref)
    slot = lax.broadcasted_iota(jnp.int32, (win, LANES), 0)
    tok_in_row = lax.broadcasted_iota(jnp.int32, (1, LANES), 1).astype(F32)
    feat = lax.broadcasted_iota(jnp.int32, (8, LANES), 0)
    nt = (((1,), (1,)), ((), ()))

    def row(r, carry):
        base = pl.multiple_of((rowstart_ref[0, 0, r] // LANES) * LANES, LANES)
        pos = pos_ref[0, 0, pl.ds(r, 1), :].astype(jnp.int32)
        a = aff_ref[0, 0, pl.ds(r, 1), :]
        onehot = jnp.where(slot + base == pos, 1.0, 0.0).astype(BF16)
        g_hi = a.astype(BF16)
        g_mid = (a - g_hi.astype(F32)).astype(BF16)
        g_lo = (a - g_hi.astype(F32) - g_mid.astype(F32)).astype(BF16)
        feats8 = jnp.where(feat == 0, g_hi.astype(F32), jnp.where(feat == 1, g_mid.astype(F32), jnp.where(
            feat == 2, g_lo.astype(F32), jnp.where(feat == 3, tok_in_row, jnp.where(feat == 4, jnp.asarray(r).astype(F32), 0.0)))))
        feats = jnp.concatenate([feats8, jnp.zeros((LANES - 8, LANES), F32)], axis=0).astype(BF16)
        o_ref[0, 0, pl.ds(base, win), :] += lax.dot_general(onehot, feats, nt, preferred_element_type=F32)
        return carry

    lax.fori_loop(0, nr, row, 0)
    o = o_ref[0, 0]
    lane = lax.broadcasted_iota(jnp.int32, o.shape, 1)
    gate = o[:, 0:1] + o[:, 1:2] + o[:, 2:3]
    token = o[:, 4:5] * LANES + o[:, 3:4]
    o_ref[0, 0] = jnp.where(lane == 0, token, jnp.where(lane == 1, gate, 0.0))


def _route(aff_tiles, cap):
    b, ne, nr, _ = aff_tiles.shape
    pos, rowstart = pl.pallas_call(
        functools.partial(_route_select_kernel, cap=cap),
        grid=(b,),
        in_specs=[pl.BlockSpec((1, ne, nr, LANES), lambda bi: (bi, 0, 0, 0))],
        out_specs=[pl.BlockSpec((1, ne, nr, LANES), lambda bi: (bi, 0, 0, 0)),
                   pl.BlockSpec((1, ne, 8, nr), lambda bi: (bi, 0, 0, 0))],
        out_shape=[jax.ShapeDtypeStruct((b, ne, nr, LANES), F32),
                   jax.ShapeDtypeStruct((b, ne, 8, nr), jnp.int32)],
        compiler_params=_cparams(("parallel",)),
        name="route_select",
    )(aff_tiles)
    rows_out = cap + 2 * LANES
    lists = pl.pallas_call(
        functools.partial(_route_compact_kernel, cap=cap),
        grid=(b, ne),
        in_specs=[pl.BlockSpec((1, 1, nr), lambda bi, e: (bi * ne + e, 0, 0), memory_space=pltpu.SMEM),
                  pl.BlockSpec((1, 1, nr, LANES), lambda bi, e: (bi, e, 0, 0)),
                  pl.BlockSpec((1, 1, nr, LANES), lambda bi, e: (bi, e, 0, 0))],
        out_specs=pl.BlockSpec((1, 1, rows_out, LANES), lambda bi, e: (bi, e, 0, 0)),
        out_shape=jax.ShapeDtypeStruct((b, ne, rows_out, LANES), F32),
        compiler_params=_cparams(("parallel", "parallel")),
        name="route_compact",
    )(rowstart[:, :, 0, :].reshape(b * ne, 1, nr), pos, aff_tiles)
    idx = lists[:, :, :cap, 0].astype(jnp.int32)
    gate = lists[:, :, :cap, 1:2]
    return idx, gate


def _moe_kernel(idx_ref, gate_ref, hm_hbm, xin_hbm, wg_ref, wu_ref, wd_ref, xout_hbm,
                rows_ref, xs_ref, acc_ref, sem, *, s, tc):
    del xin_hbm
    bi = pl.program_id(0)
    ct = pl.program_id(2)
    f = pl.program_id(3)
    nf = pl.num_programs(3)
    base = bi * s

    def token(r):
        return base + idx_ref[0, 0, ct * tc + r]

    def wait_rows(src_hbm):
        pltpu.make_async_copy(src_hbm.at[pl.ds(0, tc)], rows_ref, sem).wait()

    @pl.when(f == 0)
    def _():
        def issue(r, carry):
            pltpu.make_async_copy(hm_hbm.at[pl.ds(token(r), 1)], rows_ref.at[pl.ds(r, 1)], sem).start()
            return carry
        lax.fori_loop(0, tc, issue, 0, unroll=8)
        wait_rows(hm_hbm)
        xs_ref[...] = rows_ref[...].astype(BF16)
        acc_ref[...] = jnp.zeros_like(acc_ref)

    xs = xs_ref[...]
    a = jnp.dot(xs, wg_ref[0], preferred_element_type=F32)
    u = jnp.dot(xs, wu_ref[0], preferred_element_type=F32)
    act = (a * (1.0 / (1.0 + jnp.exp(-a))) * u).astype(BF16)
    acc_ref[...] += jnp.dot(act, wd_ref[0], preferred_element_type=F32)

    @pl.when(f == nf - 1)
    def _():
        def fetch(r, carry):
            pltpu.make_async_copy(xout_hbm.at[pl.ds(token(r), 1)], rows_ref.at[pl.ds(r, 1)], sem).start()
            return carry
        lax.fori_loop(0, tc, fetch, 0, unroll=8)
        wait_rows(xout_hbm)
        rows_ref[...] = rows_ref[...] + acc_ref[...] * gate_ref[0, 0]

        def put(r, carry):
            pltpu.make_async_copy(rows_ref.at[pl.ds(r, 1)], xout_hbm.at[pl.ds(token(r), 1)], sem).start()
            return carry
        lax.fori_loop(0, tc, put, 0, unroll=8)
        wait_rows(xout_hbm)


def _moe(idx, gate, hm, x2, w_gate, w_up, w_down, s, tc, tf):
    m, d = x2.shape
    b, e, cap = idx.shape
    ff = w_gate.shape[2]
    return pl.pallas_call(
        functools.partial(_moe_kernel, s=s, tc=tc),
        grid=(b, e, cap // tc, ff // tf),
        in_specs=[
            pl.BlockSpec((1, 1, cap), lambda bi, ei, ci, fi: (bi * e + ei, 0, 0), memory_space=pltpu.SMEM),
            pl.BlockSpec((1, 1, tc, 1), lambda bi, ei, ci, fi: (bi, ei, ci, 0)),
            pl.BlockSpec(memory_space=pl.ANY),
            pl.BlockSpec(memory_space=pl.ANY),
            pl.BlockSpec((1, d, tf), lambda bi, ei, ci, fi: (ei, 0, fi)),
            pl.BlockSpec((1, d, tf), lambda bi, ei, ci, fi: (ei, 0, fi)),
            pl.BlockSpec((1, tf, d), lambda bi, ei, ci, fi: (ei, fi, 0)),
        ],
        out_specs=pl.BlockSpec(memory_space=pl.ANY),
        out_shape=jax.ShapeDtypeStruct((m, d), F32),
        scratch_shapes=[pltpu.VMEM((tc, d), F32), pltpu.VMEM((tc, d), BF16), pltpu.VMEM((tc, d), F32),
                        pltpu.SemaphoreType.DMA(())],
        input_output_aliases={3: 0},
        compiler_params=_cparams(("arbitrary", "arbitrary", "arbitrary", "arbitrary")),
        name="moe_experts",
    )(idx.reshape(b * e, 1, cap), gate, hm, x2, w_gate, w_up, w_down)


def _final_norm_kernel(x_ref, w_ref, o_ref):
    o_ref[...] = _rms(x_ref[...], w_ref[...])


def _final_norm(x2d, w, tm):
    m, d = x2d.shape
    return pl.pallas_call(
        _final_norm_kernel,
        grid=(m // tm,),
        in_specs=[pl.BlockSpec((tm, d), lambda i: (i, 0)), pl.BlockSpec((1, d), lambda i: (0, 0))],
        out_specs=pl.BlockSpec((tm, d), lambda i: (i, 0)),
        out_shape=jax.ShapeDtypeStruct((m, d), F32),
        compiler_params=_cparams(("parallel",)),
        name="final_norm",
    )(x2d, w.reshape(1, d))


def _pick(n, pref):
    return pref if n % pref == 0 else n


def kernel(x, mem, rel_bias_table, norm_mix, w_in, conv_w, lambda_q1, lambda_k1, lambda_q2, lambda_k2,
           da_subln, gdn_a_log, gdn_dt_bias, gdn_norm, w_out, norm_cross, norm_mem, w_cq, w_ckv, w_co,
           norm_moe, w_router, w_gate, w_up, w_down, norm_final):
    b, s, d = x.shape
    m = b * s
    l = 0
    x2d = x.reshape(m, d)
    da_cols = DA_HEADS * 2 * DA_QK_DIM
    main_cols = 3 * da_cols + 4 * GDN_HEADS * GDN_D

    w_main = w_in[l, :, :main_cols].astype(BF16)
    colscale = jnp.concatenate([jnp.full((da_cols,), DA_QK_DIM ** -0.5 * LOG2E, F32),
                                jnp.ones((main_cols - da_cols,), F32)])
    proj2d = _norm_matmul(x2d, norm_mix[l], w_main, colscale, _pick(m, 1024), 1024, BF16)
    proj = proj2d.reshape(b, s, main_cols)

    w_gates = jnp.pad(w_in[l, :, main_cols:], ((0, 0), (0, LANES - 4 * GDN_HEADS)))
    zeros8 = jnp.zeros((GDN_HEADS,), F32)
    par = jnp.zeros((8, LANES), F32)
    par = par.at[0, :4 * GDN_HEADS].set(jnp.concatenate([gdn_a_log[l, 0], zeros8, gdn_a_log[l, 1], zeros8]))
    par = par.at[1, :4 * GDN_HEADS].set(jnp.concatenate([gdn_dt_bias[l, 0], zeros8, gdn_dt_bias[l, 1], zeros8]))
    gates = _gates(x2d, norm_mix[l], w_gates, par, _pick(m, 512)).reshape(b, s, LANES)

    t_att = _pick(s, ATT_TILE)
    bias_tiles = _bias_tiles(rel_bias_table.astype(F32), t_att)
    lam_vecs = jnp.zeros((8, LANES), F32)
    lam_vecs = lam_vecs.at[0, :DA_QK_DIM].set(lambda_q1[l]).at[1, :DA_QK_DIM].set(lambda_k1[l])
    lam_vecs = lam_vecs.at[2, :DA_QK_DIM].set(lambda_q2[l]).at[3, :DA_QK_DIM].set(lambda_k2[l])
    o_da = _diff_attention(rel_bias_table.astype(F32), proj, bias_tiles, lam_vecs,
                           da_subln[l].reshape(1, DA_V_DIM), t_att)

    conv_pad = jnp.pad(conv_w[l], ((0, 8 - GDN_CONV), (0, 0)))
    qkv = _gdn_qkv(proj, conv_pad, _pick(s, 512))
    o_f = _gdn_scan(qkv, gates, reverse=False)
    o_b = _gdn_scan(qkv, gates, reverse=True)

    width = GDN_HEADS * GDN_D
    x1 = _outproj(x2d, o_da.reshape(m, width), o_f.reshape(m, width), o_b.reshape(m, width), proj2d,
                  gdn_norm[l], w_out[l].astype(BF16), _pick(m, 512))

    mem_len = mem.shape[1]
    ca = CA_HEADS * CA_HEAD_DIM
    kv = _norm_matmul(mem.reshape(b * mem_len, d), norm_mem[l], w_ckv[l].astype(BF16),
                      jnp.ones((2 * ca,), F32), _pick(b * mem_len, 256), _pick(2 * ca, 512), BF16)
    w_router_pad = jnp.pad(w_router[l], ((0, 0), (0, LANES - N_EXPERTS)))
    x2, hm, aff = _cross_attention(x1, norm_cross[l], w_cq[l].astype(BF16), kv.reshape(b, mem_len, 2 * ca),
                                   w_co[l].astype(BF16), norm_moe[l], w_router_pad, s, _pick(m, 512))

    cap = CAPACITY * s // N_EXPERTS
    aff_t = jnp.swapaxes(aff.reshape(b, s, LANES)[:, :, :N_EXPERTS], 1, 2)
    idx, gate = _route(aff_t.reshape(b, N_EXPERTS, s // LANES, LANES), cap)
    x3 = _moe(idx, gate, hm, x2,
              w_gate[l].astype(BF16), w_up[l].astype(BF16), w_down[l].astype(BF16),
              s, _pick(cap, 1024), _pick(w_gate.shape[3], 512))

    return _final_norm(x3, norm_final, _pick(m, 512)).reshape(b, s, d)
```
